```python
import math
import jax
import jax.numpy as jnp
from jax import lax
import numpy as np

D_MODEL = 1024
BATCH = 8
SEQ = 4096
DEPTH = 2
DEC_BATCH = 32
DEC_SEQ = 1
PAST_LEN = 16384
PAGE_SIZE = 128

D_MIX = D_MODEL
D_ATT = D_MIX // 4
D_RWKV = D_MIX // 4
D_SSM = D_MIX // 4
D_POOL = D_MIX - D_ATT - D_RWKV - D_SSM
HEAD_DIM = 64
ATT_HEADS = D_ATT // HEAD_DIM
MOBA_BLOCK = 256
MOBA_TOPK = 3
MOBA_QBLOCK = 32
RWKV_HEADS = D_RWKV // HEAD_DIM
RWKV_DECAY_RANK = D_MODEL // 16
RWKV_A_RANK = D_MODEL // 16
RWKV_GATE_RANK = D_MODEL // 8
RWKV_FEAT = 3 * D_RWKV + RWKV_DECAY_RANK + RWKV_A_RANK + RWKV_GATE_RANK
RWKV_LN_EPS = 64e-5
SSM_GROUP = 16
SSM_GROUPS = D_SSM // SSM_GROUP
SSM_STATE = 64
DT_MIN = 1e-3
DT_MAX = 1e-1
POOL_WINDOWS = (2, 4, 8, 16)
POOL_GROUP_W = D_POOL // len(POOL_WINDOWS)
POOL_HIST = max(POOL_WINDOWS) - 1
D_FF = -(-8 * D_MODEL // (3 * 256)) * 256
D_IN = 3 * D_ATT + RWKV_FEAT + D_SSM + D_POOL
RMS_EPS = 1e-6
NEG = -1e30

kernel_name = "hybrid_moba_rwkv7_s5_pool_decode_step"


def rms_norm(x, g):
    xf = x.astype(jnp.float32)
    y = xf * lax.rsqrt(jnp.mean(xf * xf, axis=-1, keepdims=True) + RMS_EPS)
    return (y * g.astype(jnp.float32)).astype(x.dtype)


def alibi_slopes():
    return jnp.asarray([2.0 ** (-8.0 * (h + 1) / ATT_HEADS) for h in range(ATT_HEADS)], jnp.float32)


def moba_attention(q, q_pos, k, v):
    f32 = jnp.float32
    B, Lq, H, hd = q.shape
    Lk = k.shape[1]
    n_blk = -(-Lk // MOBA_BLOCK)
    pad = n_blk * MOBA_BLOCK - Lk
    kb = jnp.pad(k, ((0, 0), (0, pad), (0, 0), (0, 0))).reshape(B, n_blk, MOBA_BLOCK, H, hd).transpose(0, 3, 1, 2, 4)
    vb = jnp.pad(v, ((0, 0), (0, pad), (0, 0), (0, 0))).reshape(B, n_blk, MOBA_BLOCK, H, hd).transpose(0, 3, 1, 2, 4)
    k_mean = jnp.mean(kb.astype(f32), axis=3)
    n_sel = min(MOBA_TOPK, n_blk)
    qbs = MOBA_QBLOCK if Lq % MOBA_QBLOCK == 0 else Lq
    n_qb = Lq // qbs
    slopes = alibi_slopes()[None, :, None, None, None]
    scale = hd ** -0.5
    bi = jnp.arange(B)[:, None, None, None]
    hi = jnp.arange(H)[None, :, None, None]
    blk_ids = jnp.arange(n_blk, dtype=jnp.int32)
    offs = jnp.arange(MOBA_BLOCK, dtype=jnp.int32)

    def attend_block(args):
        qb, pb = args
        own = pb // MOBA_BLOCK
        gate = jnp.einsum('bhqd,bhnd->bhqn', qb.astype(f32), k_mean)
        gate = jnp.where(blk_ids[None, :] < own[:, None], gate, NEG)
        _, top = lax.top_k(gate, n_sel)
        idx = jnp.concatenate([top, jnp.broadcast_to(own[None, None, :, None], (B, H, qbs, 1))], axis=-1)
        sel_ok = jnp.arange(n_sel)[None, :] < jnp.minimum(own, MOBA_TOPK)[:, None]
        ok = jnp.concatenate([sel_ok, jnp.ones((qbs, 1), bool)], axis=-1)
        kg = kb[bi, hi, idx]
        vg = vb[bi, hi, idx]
        kpos = idx[..., None] * MOBA_BLOCK + offs
        dist = (pb[None, None, :, None, None] - kpos).astype(f32)
        s = jnp.einsum('bhqd,bhqnkd->bhqnk', qb, kg).astype(f32) * scale - slopes * dist
        s = jnp.where((dist >= 0) & ok[None, None, :, :, None], s, NEG)
        p = jax.nn.softmax(s.reshape(B, H, qbs, -1), axis=-1).reshape(s.shape)
        return jnp.einsum('bhqnk,bhqnkd->bhqd', p.astype(vg.dtype), vg)

    qs = q.transpose(0, 2, 1, 3).reshape(B, H, n_qb, qbs, hd).transpose(2, 0, 1, 3, 4)
    ps = q_pos.reshape(n_qb, qbs)
    out = lax.map(attend_block, (qs, ps))
    return out.transpose(1, 0, 3, 2, 4).reshape(B, Lq, H * hd)


def rwkv7_mix(f, shift0, wkv0, lp):
    f32 = jnp.float32
    B, L, _ = f.shape
    H, N = RWKV_HEADS, HEAD_DIM
    ff = f.astype(f32)
    prev = jnp.concatenate([shift0[:, None].astype(f32), ff[:, :-1]], axis=1)
    m = ff + (prev - ff) * lp['mu_shift'].astype(f32)
    r = m[..., :D_RWKV]
    k = m[..., D_RWKV:2 * D_RWKV]
    v = m[..., 2 * D_RWKV:3 * D_RWKV]
    c0 = 3 * D_RWKV
    w_lo = m[..., c0:c0 + RWKV_DECAY_RANK]
    a_lo = m[..., c0 + RWKV_DECAY_RANK:c0 + RWKV_DECAY_RANK + RWKV_A_RANK]
    g_lo = m[..., c0 + RWKV_DECAY_RANK + RWKV_A_RANK:]
    w_log = -jax.nn.softplus(-(lp['w0'] + jnp.tanh(w_lo) @ lp['w2'])) - 0.5
    decay = jnp.exp(-jnp.exp(w_log.astype(f32)))
    a = jax.nn.sigmoid(lp['a0'] + a_lo @ lp['a2']).astype(f32)
    g = (jax.nn.sigmoid(g_lo) @ lp['g2']).astype(f32)
    kk = (k * lp['k_k']).reshape(B, L, H, N)
    kk = kk * lax.rsqrt(jnp.maximum(jnp.sum(kk * kk, axis=-1, keepdims=True), 1e-12))
    k = k * (1.0 + (a - 1.0) * lp['k_a'])
    r, k, v, decay, a = (t.astype(f32).reshape(B, L, H, N) for t in (r, k, v, decay, a))
    kk = kk.astype(f32)

    def step(S, inp):
        r_t, k_t, v_t, w_t, kk_t, a_t = inp
        s_kk = jnp.einsum('bhvk,bhk->bhv', S, kk_t)
        S = S * w_t[:, :, None, :] - s_kk[..., None] * (kk_t * a_t)[:, :, None, :] + v_t[..., None] * k_t[:, :, None, :]
        return S, jnp.einsum('bhvk,bhk->bhv', S, r_t)

    seq = tuple(jnp.moveaxis(t, 1, 0) for t in (r, k, v, decay, kk, a))
    S_fin, o = lax.scan(step, wkv0.astype(f32), seq)
    o = jnp.moveaxis(o, 0, 1)
    mu = jnp.mean(o, axis=-1, keepdims=True)
    var = jnp.mean(jnp.square(o - mu), axis=-1, keepdims=True)
    o = ((o - mu) * lax.rsqrt(var + RWKV_LN_EPS)).reshape(B, L, D_RWKV) * lp['lnx_g'] + lp['lnx_b']
    bonus = jnp.sum(r * k * lp['r_k'].astype(f32).reshape(H, N), axis=-1, keepdims=True) * v
    o = (o + bonus.reshape(B, L, D_RWKV)) * g
    return o.astype(f.dtype), f[:, -1], S_fin.astype(wkv0.dtype)


def s5_mix(u, h_re, h_im, lp):
    f32 = jnp.float32
    B, L, _ = u.shape
    uf = u.astype(f32).reshape(B, L, SSM_GROUPS, SSM_GROUP)
    lam = lax.complex(lp['lam_re'].astype(f32), lp['lam_im'].astype(f32))
    step = jnp.exp(lp['log_dt'].astype(f32))[:, None]
    a_bar = jnp.exp(lam * step)
    b_bar = ((a_bar - 1.0) / lam)[..., None] * lax.complex(lp['b_re'].astype(f32), lp['b_im'].astype(f32))
    bu = jnp.einsum('gnp,blgp->blgn', b_bar, uf.astype(jnp.complex64))
    h0 = lax.complex(h_re.astype(f32), h_im.astype(f32))
    bu = bu.at[:, 0].add(a_bar * h0)

    def combine(left, right):
        a1, b1 = left
        a2, b2 = right
        return a1 * a2, a2 * b1 + b2

    _, hs = lax.associative_scan(combine, (jnp.broadcast_to(a_bar, bu.shape), bu), axis=1)
    cm = lax.complex(lp['c_re'].astype(f32), lp['c_im'].astype(f32))
    y = jnp.real(jnp.einsum('gpn,blgn->blgp', cm, hs)) + lp['d_skip'].astype(f32).reshape(SSM_GROUPS, SSM_GROUP) * uf
    y = jax.nn.gelu(y.reshape(B, L, D_SSM))
    y = y * jax.nn.sigmoid(y @ lp['w_glu'].astype(f32) + lp['b_glu'].astype(f32))
    h_last = hs[:, -1]
    return y.astype(u.dtype), jnp.real(h_last).astype(h_re.dtype), jnp.imag(h_last).astype(h_im.dtype)


def pool_mix(u, hist, pos0, lp):
    f32 = jnp.float32
    B, L, _ = u.shape
    ext = jnp.concatenate([hist.astype(f32), u.astype(f32)], axis=1)
    cs = jnp.concatenate([jnp.zeros((B, 1, D_POOL), f32), jnp.cumsum(ext, axis=1)], axis=1)
    pos = pos0 + jnp.arange(L, dtype=jnp.int32)
    means = []
    for gi, w in enumerate(POOL_WINDOWS):
        cols = slice(gi * POOL_GROUP_W, (gi + 1) * POOL_GROUP_W)
        win_sum = cs[:, POOL_HIST + 1:POOL_HIST + 1 + L, cols] - cs[:, POOL_HIST + 1 - w:POOL_HIST + 1 - w + L, cols]
        count = jnp.minimum(pos + 1, w).astype(f32)[None, :, None]
        means.append(win_sum / count)
    pooled = jnp.concatenate(means, axis=-1) - u.astype(f32)
    z = jnp.einsum('blgc,gcd->blgd', pooled.reshape(B, L, len(POOL_WINDOWS), POOL_GROUP_W), lp['w_pool'].astype(f32))
    z = z.reshape(B, L, D_POOL) * lp['pool_scale'].astype(f32)
    return z.astype(u.dtype), ext[:, -POOL_HIST:].astype(hist.dtype)


def mixer_block(h, pos0, k_past, v_past, wkv0, shift0, ssm_re0, ssm_im0, pool0, lp):
    B, L, _ = h.shape
    z = h @ lp['w_in']
    q = z[..., :D_ATT].reshape(B, L, ATT_HEADS, HEAD_DIM)
    k = z[..., D_ATT:2 * D_ATT].reshape(B, L, ATT_HEADS, HEAD_DIM)
    v = z[..., 2 * D_ATT:3 * D_ATT].reshape(B, L, ATT_HEADS, HEAD_DIM)
    c0 = 3 * D_ATT
    f_rwkv = z[..., c0:c0 + RWKV_FEAT]
    u_ssm = z[..., c0 + RWKV_FEAT:c0 + RWKV_FEAT + D_SSM]
    u_pool = z[..., c0 + RWKV_FEAT + D_SSM:]
    if k_past is None:
        k_all, v_all = k, v
    else:
        k_all = jnp.concatenate([k_past.astype(k.dtype), k], axis=1)
        v_all = jnp.concatenate([v_past.astype(v.dtype), v], axis=1)
    q_pos = pos0 + jnp.arange(L, dtype=jnp.int32)
    att = moba_attention(q, q_pos, k_all, v_all).astype(h.dtype)
    rw, shift_new, wkv_new = rwkv7_mix(f_rwkv, shift0, wkv0, lp)
    ss, ssm_re, ssm_im = s5_mix(u_ssm, ssm_re0, ssm_im0, lp)
    pl, pool_new = pool_mix(u_pool, pool0, pos0, lp)
    g = lp['mix_g']
    e1, e2, e3 = D_ATT, D_ATT + D_RWKV, D_ATT + D_RWKV + D_SSM
    merged = jnp.concatenate([rms_norm(att, g[:e1]), rms_norm(rw.astype(h.dtype), g[e1:e2]),
                              rms_norm(ss.astype(h.dtype), g[e2:e3]), rms_norm(pl.astype(h.dtype), g[e3:])], axis=-1)
    return merged @ lp['w_out'], k, v, wkv_new, shift_new, ssm_re, ssm_im, pool_new


def swiglu(h, lp):
    return (jax.nn.silu(h @ lp['w_gate']) * (h @ lp['w_up'])) @ lp['w_down']


def run_trunk(x, pos0, cache_k, cache_v, page_table, wkv0, shift0, ssm_re0, ssm_im0, pool0, P, final_g):
    Bn = x.shape[0]
    per_layer = []
    for l in range(DEPTH):
        lp = {name: arr[l] for name, arr in P.items()}
        if cache_k is None:
            k_past = v_past = None
        else:
            k_past = cache_k[l][page_table].reshape(Bn, -1, ATT_HEADS, HEAD_DIM)
            v_past = cache_v[l][page_table].reshape(Bn, -1, ATT_HEADS, HEAD_DIM)
        mo, *new = mixer_block(rms_norm(x, lp['norm1_g']), pos0, k_past, v_past,
                               wkv0[l], shift0[l], ssm_re0[l], ssm_im0[l], pool0[l], lp)
        x = x + mo
        x = x + swiglu(rms_norm(x, lp['norm2_g']), lp)
        per_layer.append(new)
    new_state = [jnp.stack(s, axis=0) for s in zip(*per_layer)]
    return rms_norm(x, final_g), new_state


def setup_inputs(seed: int = 0) -> dict:
    key = jax.random.key(seed)
    ks = iter(jax.random.split(key, 64))
    f32 = jnp.float32

    def nrm(shape, s=1.0):
        return s * jax.random.normal(next(ks), shape, f32)

    def unif(shape, lo=0.0, hi=1.0):
        return jax.random.uniform(next(ks), shape, f32, lo, hi)

    n_pages = PAST_LEN // PAGE_SIZE
    n_used = DEC_BATCH * n_pages
    n_pool = n_used + max(1, n_used // 4)
    L = DEPTH
    inp = {}
    inp['x_prompt'] = nrm((BATCH, SEQ, D_MODEL))
    inp['x_sample'] = nrm((DEC_BATCH, DEC_SEQ, D_MODEL))
    inp['cache_k'] = nrm((L, n_pool, PAGE_SIZE, ATT_HEADS, HEAD_DIM))
    inp['cache_v'] = nrm((L, n_pool, PAGE_SIZE, ATT_HEADS, HEAD_DIM))
    inp['page_table'] = jax.random.permutation(next(ks), n_pool)[:n_used].reshape(DEC_BATCH, n_pages).astype(jnp.int32)
    inp['state_wkv'] = nrm((L, DEC_BATCH, RWKV_HEADS, HEAD_DIM, HEAD_DIM), 0.3)
    inp['state_shift'] = nrm((L, DEC_BATCH, RWKV_FEAT))
    inp['state_ssm_re'] = nrm((L, DEC_BATCH, SSM_GROUPS, SSM_STATE), 0.3)
    inp['state_ssm_im'] = nrm((L, DEC_BATCH, SSM_GROUPS, SSM_STATE), 0.3)
    inp['state_pool'] = nrm((L, DEC_BATCH, POOL_HIST, D_POOL))
    inp['norm1_g'] = 1.0 + nrm((L, D_MODEL), 0.05)
    inp['w_in'] = nrm((L, D_MODEL, D_IN), D_MODEL ** -0.5)
    inp['mu_shift'] = unif((L, RWKV_FEAT))
    inp['w0'] = jnp.tile(jnp.linspace(-6.5, -1.5, HEAD_DIM), RWKV_HEADS)[None] + nrm((L, D_RWKV), 0.1)
    inp['w2'] = nrm((L, RWKV_DECAY_RANK, D_RWKV), 0.1)
    inp['a0'] = nrm((L, D_RWKV), 0.1)
    inp['a2'] = nrm((L, RWKV_A_RANK, D_RWKV), 0.02)
    inp['g2'] = nrm((L, RWKV_GATE_RANK, D_RWKV), RWKV_GATE_RANK ** -0.5)
    inp['k_k'] = 0.85 + nrm((L, D_RWKV), 0.05)
    inp['k_a'] = 1.0 + nrm((L, D_RWKV), 0.05)
    inp['r_k'] = nrm((L, D_RWKV), 0.1)
    inp['lnx_g'] = 1.0 + nrm((L, D_RWKV), 0.05)
    inp['lnx_b'] = nrm((L, D_RWKV), 0.01)
    inp['lam_re'] = -0.5 + nrm((L, SSM_GROUPS, SSM_STATE), 0.01)
    inp['lam_im'] = jnp.pi * jnp.arange(SSM_STATE, dtype=f32)[None, None] + nrm((L, SSM_GROUPS, SSM_STATE), 0.01)
    inp['log_dt'] = unif((L, SSM_GROUPS), math.log(DT_MIN), math.log(DT_MAX))
    inp['b_re'] = nrm((L, SSM_GROUPS, SSM_STATE, SSM_GROUP), (2 * SSM_GROUP) ** -0.5)
    inp['b_im'] = nrm((L, SSM_GROUPS, SSM_STATE, SSM_GROUP), (2 * SSM_GROUP) ** -0.5)
    inp['c_re'] = nrm((L, SSM_GROUPS, SSM_GROUP, SSM_STATE), (2 * SSM_STATE) ** -0.5)
    inp['c_im'] = nrm((L, SSM_GROUPS, SSM_GROUP, SSM_STATE), (2 * SSM_STATE) ** -0.5)
    inp['d_skip'] = nrm((L, D_SSM))
    inp['w_glu'] = nrm((L, D_SSM, D_SSM), D_SSM ** -0.5)
    inp['b_glu'] = nrm((L, D_SSM), 0.01)
    inp['w_pool'] = nrm((L, len(POOL_WINDOWS), POOL_GROUP_W, POOL_GROUP_W), POOL_GROUP_W ** -0.5)
    inp['pool_scale'] = 1.0 + nrm((L, D_POOL), 0.1)
    inp['mix_g'] = 1.0 + nrm((L, D_MIX), 0.05)
    inp['w_out'] = nrm((L, D_MIX, D_MODEL), D_MIX ** -0.5)
    inp['norm2_g'] = 1.0 + nrm((L, D_MODEL), 0.05)
    inp['w_gate'] = nrm((L, D_MODEL, D_FF), D_MODEL ** -0.5)
    inp['w_up'] = nrm((L, D_MODEL, D_FF), D_MODEL ** -0.5)
    inp['w_down'] = nrm((L, D_FF, D_MODEL), D_FF ** -0.5)
    inp['final_g'] = 1.0 + nrm((D_MODEL,), 0.05)
    return inp


def reference(x_prompt, x_sample, cache_k, cache_v, page_table, state_wkv, state_shift, state_ssm_re,
              state_ssm_im, state_pool, norm1_g, w_in, mu_shift, w0, w2, a0, a2, g2, k_k, k_a, r_k,
              lnx_g, lnx_b, lam_re, lam_im, log_dt, b_re, b_im, c_re, c_im, d_skip, w_glu, b_glu,
              w_pool, pool_scale, mix_g, w_out, norm2_g, w_gate, w_up, w_down, final_g):
    P = dict(norm1_g=norm1_g, w_in=w_in, mu_shift=mu_shift, w0=w0, w2=w2, a0=a0, a2=a2, g2=g2,
             k_k=k_k, k_a=k_a, r_k=r_k, lnx_g=lnx_g, lnx_b=lnx_b, lam_re=lam_re, lam_im=lam_im,
             log_dt=log_dt, b_re=b_re, b_im=b_im, c_re=c_re, c_im=c_im, d_skip=d_skip, w_glu=w_glu,
             b_glu=b_glu, w_pool=w_pool, pool_scale=pool_scale, mix_g=mix_g, w_out=w_out,
             norm2_g=norm2_g, w_gate=w_gate, w_up=w_up, w_down=w_down)
    dt = x_prompt.dtype
    bp = x_prompt.shape[0]
    y_prompt, (k_p, v_p, wkv_p, shift_p, ssm_re_p, ssm_im_p, pool_p) = run_trunk(
        x_prompt, 0, None, None, None,
        jnp.zeros((DEPTH, bp, RWKV_HEADS, HEAD_DIM, HEAD_DIM), dt),
        jnp.zeros((DEPTH, bp, RWKV_FEAT), dt),
        jnp.zeros((DEPTH, bp, SSM_GROUPS, SSM_STATE), dt),
        jnp.zeros((DEPTH, bp, SSM_GROUPS, SSM_STATE), dt),
        jnp.zeros((DEPTH, bp, POOL_HIST, D_POOL), dt), P, final_g)
    past_len = page_table.shape[1] * cache_k.shape[2]
    y_sample, (k_s, v_s, wkv_s, shift_s, ssm_re_s, ssm_im_s, pool_s) = run_trunk(
        x_sample, past_len, cache_k, cache_v, page_table, state_wkv, state_shift,
        state_ssm_re, state_ssm_im, state_pool, P, final_g)
    return (y_prompt, y_sample, k_p, v_p, k_s, v_s, wkv_p, wkv_s, shift_p, shift_s,
            ssm_re_p, ssm_im_p, ssm_re_s, ssm_im_s, pool_p, pool_s)
```

```python
import functools
import math

import jax
import jax.numpy as jnp
from jax import lax
from jax.experimental import pallas as pl
from jax.experimental.pallas import tpu as pltpu

F32 = jnp.float32
BF16 = jnp.bfloat16
HI = lax.Precision.HIGHEST

HEAD_DIM = 64
N_HEADS = 4
D_GRP = N_HEADS * HEAD_DIM
MOBA_BLOCK = 256
MOBA_TOPK = 3
RWKV_LN_EPS = 64e-5
RMS_EPS = 1e-6
NEG = -1e30
SSM_GROUP = 16
SSM_STATE = 64
POOL_WINDOWS = (2, 4, 8, 16)
POOL_HIST = 15
ALIBI_SLOPES = tuple(2.0 ** (-8.0 * (h + 1) / N_HEADS) for h in range(N_HEADS))
VMEM_LIMIT = 56 * 1024 * 1024


def _params(sem):
    return pltpu.CompilerParams(dimension_semantics=sem, vmem_limit_bytes=VMEM_LIMIT)


def _mm(a, b, bf16):
    if bf16:
        return jnp.dot(a.astype(BF16), b.astype(BF16), preferred_element_type=F32)
    return jnp.dot(a, b, precision=HI, preferred_element_type=F32)


def _mm_nt(a, b, bf16):
    dn = (((1,), (1,)), ((), ()))
    if bf16:
        return lax.dot_general(a.astype(BF16), b.astype(BF16), dn, preferred_element_type=F32)
    return lax.dot_general(a, b, dn, precision=HI, preferred_element_type=F32)


def _rms(x, g):
    return x * lax.rsqrt(jnp.mean(x * x, axis=-1, keepdims=True) + RMS_EPS) * g


def _sigmoid(x):
    return 1.0 / (1.0 + jnp.exp(-x))


def _row_tile(n, target):
    t = min(n, target)
    while n % t:
        t //= 2
    return t


def _inproj_kernel(x_ref, g_ref, w_ref, *out_refs, widths):
    h = _rms(x_ref[...], g_ref[...]).astype(BF16)
    c0 = 0
    for o_ref, wd in zip(out_refs, widths):
        o_ref[...] = jnp.dot(h, w_ref[:, c0:c0 + wd], preferred_element_type=F32)
        c0 += wd


def _inproj(x, g, w_bf16, widths):
    n, d = x.shape
    tm = _row_tile(n, 512)
    return pl.pallas_call(
        functools.partial(_inproj_kernel, widths=widths),
        grid=(n // tm,),
        in_specs=[pl.BlockSpec((tm, d), lambda i: (i, 0)),
                  pl.BlockSpec((1, d), lambda i: (0, 0)),
                  pl.BlockSpec(w_bf16.shape, lambda i: (0, 0))],
        out_specs=[pl.BlockSpec((tm, wd), lambda i: (i, 0)) for wd in widths],
        out_shape=[jax.ShapeDtypeStruct((n, wd), F32) for wd in widths],
        compiler_params=_params(("parallel",)),
        name="inproj",
    )(x, g.reshape(1, d), w_bf16)


def _post_kernel(att_ref, rw_ref, ss_ref, pm_ref, x_ref, mixg_ref, wout_ref, n2g_ref,
                 wg_ref, wu_ref, wd_ref, fing_ref, x_out_ref, y_out_ref):
    acc = x_ref[...]
    for i, ref in enumerate((att_ref, rw_ref, ss_ref, pm_ref)):
        c = slice(i * D_GRP, (i + 1) * D_GRP)
        a = _rms(ref[...], mixg_ref[:, c]).astype(BF16)
        acc = acc + jnp.dot(a, wout_ref[c, :], preferred_element_type=F32)
    h2 = _rms(acc, n2g_ref[...]).astype(BF16)
    gt = jnp.dot(h2, wg_ref[...], preferred_element_type=F32)
    up = jnp.dot(h2, wu_ref[...], preferred_element_type=F32)
    act = (gt * _sigmoid(gt) * up).astype(BF16)
    acc = acc + jnp.dot(act, wd_ref[...], preferred_element_type=F32)
    x_out_ref[...] = acc
    y_out_ref[...] = _rms(acc, fing_ref[...])


def _post(att, rw, ss, pm, x, mix_g, wout, n2g, wg, wu, wd, fin_g):
    n, d = x.shape
    tm = _row_tile(n, 256)
    row = lambda w: pl.BlockSpec((tm, w), lambda i: (i, 0))
    full = lambda a: pl.BlockSpec(a.shape, lambda i: (0, 0))
    vec = lambda: pl.BlockSpec((1, d), lambda i: (0, 0))
    return pl.pallas_call(
        _post_kernel,
        grid=(n // tm,),
        in_specs=[row(D_GRP), row(D_GRP), row(D_GRP), row(D_GRP), row(d), vec(), full(wout), vec(),
                  full(wg), full(wu), full(wd), vec()],
        out_specs=[row(d), row(d)],
        out_shape=[jax.ShapeDtypeStruct((n, d), F32), jax.ShapeDtypeStruct((n, d), F32)],
        compiler_params=_params(("parallel",)),
        name="post",
    )(att, rw, ss, pm, x, mix_g.reshape(1, d), wout, n2g.reshape(1, d), wg, wu, wd, fin_g.reshape(1, d))


def _attn_kernel(q_ref, k_ref, v_ref, o_ref, kmean_ref, *, nblk):
    qi = pl.program_id(1)
    blk = MOBA_BLOCK

    @pl.when(qi == 0)
    def _():
        for j in range(nblk):
            kmean_ref[j:j + 1, :] = jnp.mean(k_ref[0, j * blk:(j + 1) * blk, :], axis=0, keepdims=True)

    rowcol = (lax.broadcasted_iota(jnp.int32, (blk, blk), 0)
              - lax.broadcasted_iota(jnp.int32, (blk, blk), 1)).astype(F32)
    blk_iota = lax.broadcasted_iota(jnp.int32, (blk, nblk), 1)
    lane = lax.broadcasted_iota(jnp.int32, (blk, 2 * HEAD_DIM), 1)

    for hp in range(N_HEADS // 2):
        cs = slice(hp * 2 * HEAD_DIM, (hp + 1) * 2 * HEAD_DIM)
        q2 = q_ref[0, :, cs]
        km2 = kmean_ref[:, cs]
        qms, sels = [], []
        for hh in range(2):
            in_head = (lane // HEAD_DIM) == hh
            qm = jnp.where(in_head, q2, 0.0)
            gate = _mm_nt(qm, km2, False)
            g = jnp.where(blk_iota < qi, gate, NEG)
            sel = blk_iota == qi
            for r in range(MOBA_TOPK):
                m = jnp.max(g, axis=-1, keepdims=True)
                idx = jnp.min(jnp.where(g == m, blk_iota, nblk), axis=-1, keepdims=True)
                hit = blk_iota == idx
                sel = sel | (hit & (qi > r))
                g = jnp.where(hit, -jnp.inf, g)
            sels.append(sel.astype(F32))
            qms.append((qm * (HEAD_DIM ** -0.5)).astype(BF16))

        def body(j, carry):
            r0 = pl.multiple_of(j * blk, blk)
            kj = k_ref[0, pl.ds(r0, blk), cs].astype(BF16)
            vj = v_ref[0, pl.ds(r0, blk), cs].astype(BF16)
            dist = rowcol + ((qi - j) * blk).astype(F32)
            causal = dist >= 0.0
            new = []
            for hh in range(2):
                m_i, l_i, acc = carry[hh]
                s = _mm_nt(qms[hh], kj, True) - ALIBI_SLOPES[2 * hp + hh] * dist
                selj = jnp.sum(jnp.where(blk_iota == j, sels[hh], 0.0), axis=-1, keepdims=True)
                mask = causal & (selj > 0.0)
                s = jnp.where(mask, s, NEG)
                m_new = jnp.maximum(m_i, jnp.max(s, axis=-1, keepdims=True))
                p = jnp.where(mask, jnp.exp(s - m_new), 0.0)
                alpha = jnp.exp(m_i - m_new)
                l_new = alpha * l_i + jnp.sum(p, axis=-1, keepdims=True)
                acc_new = alpha * acc + jnp.dot(p.astype(BF16), vj, preferred_element_type=F32)
                new.append((m_new, l_new, acc_new))
            return tuple(new)

        init = tuple((jnp.full((blk, 1), NEG, F32), jnp.zeros((blk, 1), F32),
                      jnp.zeros((blk, 2 * HEAD_DIM), F32)) for _ in range(2))
        res = lax.fori_loop(0, qi + 1, body, init)
        o0 = res[0][2] / res[0][1]
        o1 = res[1][2] / res[1][1]
        o_ref[0, :, cs] = jnp.where((lane // HEAD_DIM) == 0, o0, o1)


def _attn_prompt(q, k, v):
    b, l, d = q.shape
    nblk = l // MOBA_BLOCK
    return pl.pallas_call(
        functools.partial(_attn_kernel, nblk=nblk),
        grid=(b, nblk),
        in_specs=[pl.BlockSpec((1, MOBA_BLOCK, d), lambda i, j: (i, j, 0)),
                  pl.BlockSpec((1, l, d), lambda i, j: (i, 0, 0)),
                  pl.BlockSpec((1, l, d), lambda i, j: (i, 0, 0))],
        out_specs=pl.BlockSpec((1, MOBA_BLOCK, d), lambda i, j: (i, j, 0)),
        out_shape=jax.ShapeDtypeStruct((b, l, d), F32),
        scratch_shapes=[pltpu.VMEM((nblk, d), F32)],
        compiler_params=_params(("parallel", "arbitrary")),
        name="attn_prompt",
    )(q, k, v)


def _pagesum_kernel(c_ref, o_ref):
    o_ref[0] = jnp.sum(c_ref[0], axis=1)


def _page_sums(cache):
    nl, npool, psz, d = cache.shape
    pb = _row_tile(npool, 64)
    return pl.pallas_call(
        _pagesum_kernel,
        grid=(nl, npool // pb),
        in_specs=[pl.BlockSpec((1, pb, psz, d), lambda l, i: (l, i, 0, 0))],
        out_specs=pl.BlockSpec((1, pb, d), lambda l, i: (l, i, 0)),
        out_shape=jax.ShapeDtypeStruct((nl, npool, d), F32),
        compiler_params=_params(("parallel", "parallel")),
        name="page_sums",
    )(cache)


def _select_kernel(pt_ref, q_ref, ps_ref, o_ref, kmean_ref, *, nblk, pages_per_blk, inv_blk):
    b = pl.program_id(0)

    def fill(j, _):
        acc = ps_ref[0, pl.ds(pt_ref[b, j * pages_per_blk], 1), :]
        for i in range(1, pages_per_blk):
            acc = acc + ps_ref[0, pl.ds(pt_ref[b, j * pages_per_blk + i], 1), :]
        kmean_ref[pl.ds(j, 1), :] = acc * inv_blk
        return 0

    lax.fori_loop(0, nblk, fill, 0)
    prod = kmean_ref[...] * q_ref[0]
    ind = (lax.broadcasted_iota(jnp.int32, (D_GRP, 128), 0) // HEAD_DIM
           == lax.broadcasted_iota(jnp.int32, (D_GRP, 128), 1)).astype(F32)
    g = _mm(prod, ind, False)
    bi = lax.broadcasted_iota(jnp.int32, (nblk, 128), 0)
    rows = []
    for r in range(MOBA_TOPK):
        m = jnp.max(g, axis=0, keepdims=True)
        idx = jnp.min(jnp.where(g == m, bi, nblk), axis=0, keepdims=True)
        rows.append(idx)
        g = jnp.where(bi == idx, -jnp.inf, g)
    rows.append(jnp.zeros((8 - MOBA_TOPK, 128), jnp.int32))
    o_ref[0] = jnp.concatenate(rows, axis=0)


def _select_blocks(page_table, q, psum_l, page_size):
    bsz, npg = page_table.shape
    pages_per_blk = MOBA_BLOCK // page_size
    nblk = npg // pages_per_blk
    npool = psum_l.shape[1]
    out = pl.pallas_call(
        functools.partial(_select_kernel, nblk=nblk, pages_per_blk=pages_per_blk, inv_blk=1.0 / MOBA_BLOCK),
        grid_spec=pltpu.PrefetchScalarGridSpec(
            num_scalar_prefetch=1,
            grid=(bsz,),
            in_specs=[pl.BlockSpec((1, 1, D_GRP), lambda b, pt: (b, 0, 0)),
                      pl.BlockSpec((1, npool, D_GRP), lambda b, pt: (0, 0, 0))],
            out_specs=pl.BlockSpec((1, 8, 128), lambda b, pt: (b, 0, 0)),
            scratch_shapes=[pltpu.VMEM((nblk, D_GRP), F32)]),
        out_shape=jax.ShapeDtypeStruct((bsz, 8, 128), jnp.int32),
        compiler_params=_params(("arbitrary",)),
        name="select_blocks",
    )(page_table, q.reshape(bsz, 1, D_GRP), psum_l)
    return out[:, :MOBA_TOPK, :N_HEADS]


def _decode_kernel(top_ref, pt_ref, q_ref, kn_ref, vn_ref, *refs, page_size, past_len, n_steps):
    k_refs = refs[:N_HEADS]
    v_refs = refs[N_HEADS:2 * N_HEADS]
    o_ref = refs[2 * N_HEADS]
    m_ref, l_ref, acc_ref = refs[2 * N_HEADS + 1:]
    b = pl.program_id(0)
    s_id = pl.program_id(1)
    pages_per_blk = MOBA_BLOCK // page_size
    scale = HEAD_DIM ** -0.5
    q = q_ref[0]

    @pl.when(s_id == 0)
    def _():
        for h in range(N_HEADS):
            c = slice(h * HEAD_DIM, (h + 1) * HEAD_DIM)
            m_ref[h] = jnp.sum(q[:, c] * kn_ref[0][:, c], axis=-1, keepdims=True) * scale
            l_ref[h] = jnp.ones((1, 1), F32)
            acc_ref[h] = vn_ref[0][:, c]

    row = lax.broadcasted_iota(jnp.int32, (page_size, 1), 0)
    for h in range(N_HEADS):
        c = slice(h * HEAD_DIM, (h + 1) * HEAD_DIM)
        blk = top_ref[b, (s_id // pages_per_blk) * N_HEADS + h]
        kpos0 = blk * MOBA_BLOCK + (s_id % pages_per_blk) * page_size
        dist = (past_len - kpos0 - row).astype(F32)
        kh = k_refs[h][0, 0, :, c]
        vh = v_refs[h][0, 0, :, c]
        s = jnp.sum(kh * q[:, c], axis=-1, keepdims=True) * scale - ALIBI_SLOPES[h] * dist
        m_i = m_ref[h]
        m_new = jnp.maximum(m_i, jnp.max(s, axis=0, keepdims=True))
        p = jnp.exp(s - m_new)
        alpha = jnp.exp(m_i - m_new)
        l_ref[h] = alpha * l_ref[h] + jnp.sum(p, axis=0, keepdims=True)
        acc_ref[h] = alpha * acc_ref[h] + jnp.sum(p * vh, axis=0, keepdims=True)
        m_ref[h] = m_new

    @pl.when(s_id == n_steps - 1)
    def _():
        for h in range(N_HEADS):
            o_ref[0, :, h * HEAD_DIM:(h + 1) * HEAD_DIM] = acc_ref[h] / l_ref[h]


def _attn_decode(top, page_table, q, k_new, v_new, cache_k, cache_v, layer):
    bsz, npg = page_table.shape
    page_size = cache_k.shape[2]
    pages_per_blk = MOBA_BLOCK // page_size
    n_steps = MOBA_TOPK * pages_per_blk
    past_len = npg * page_size

    def page_spec(h):
        def imap(b, s, top_ref, pt_ref):
            blk = top_ref[b, (s // pages_per_blk) * N_HEADS + h]
            return (layer, pt_ref[b, blk * pages_per_blk + s % pages_per_blk], 0, 0)
        return pl.BlockSpec((1, 1, page_size, D_GRP), imap)

    row_spec = pl.BlockSpec((1, 1, D_GRP), lambda b, s, t, p: (b, 0, 0))
    r3 = lambda a: a.reshape(bsz, 1, D_GRP)
    out = pl.pallas_call(
        functools.partial(_decode_kernel, page_size=page_size, past_len=past_len, n_steps=n_steps),
        grid_spec=pltpu.PrefetchScalarGridSpec(
            num_scalar_prefetch=2,
            grid=(bsz, n_steps),
            in_specs=[row_spec, row_spec, row_spec] + [page_spec(h) for h in range(N_HEADS)] * 2,
            out_specs=row_spec,
            scratch_shapes=[pltpu.VMEM((N_HEADS, 1, 1), F32), pltpu.VMEM((N_HEADS, 1, 1), F32),
                            pltpu.VMEM((N_HEADS, 1, HEAD_DIM), F32)]),
        out_shape=jax.ShapeDtypeStruct((bsz, 1, D_GRP), F32),
        compiler_params=_params(("arbitrary", "arbitrary")),
        name="attn_decode",
    )(top.reshape(bsz, MOBA_TOPK * N_HEADS), page_table, r3(q), r3(k_new), r3(v_new),
      *([cache_k] * N_HEADS), *([cache_v] * N_HEADS))
    return out.reshape(bsz, D_GRP)


def _rwkv_kernel(f_ref, sh0_ref, s0_ref, mu_ref, w0_ref, w2_ref, a0_ref, a2_ref, g2_ref, kk_ref, ka_ref,
                 rk_ref, lg_ref, lb_ref, o_ref, s_out_ref, s_ref, prev_ref, *, chunk, valid_len, bf16):
    i = pl.program_id(1)
    tt = f_ref.shape[1]
    n_chunks = tt // chunk
    ht = N_HEADS * chunk

    @pl.when(i == 0)
    def _():
        s_ref[...] = s0_ref[0]
        prev_ref[...] = sh0_ref[0]

    f = f_ref[0]
    row = lax.broadcasted_iota(jnp.int32, (tt, 1), 0)
    prev = jnp.where(row == 0, prev_ref[...], pltpu.roll(f, 1, 0))
    prev_ref[...] = f[tt - 1:tt, :]
    m = f + (prev - f) * mu_ref[...]
    r = m[:, 0:D_GRP]
    k = m[:, D_GRP:2 * D_GRP]
    v = m[:, 2 * D_GRP:3 * D_GRP]
    c0 = 3 * D_GRP
    dr = w2_ref.shape[0]
    ar = a2_ref.shape[0]
    w_lo = m[:, c0:c0 + dr]
    a_lo = m[:, c0 + dr:c0 + dr + ar]
    g_lo = m[:, c0 + dr + ar:]
    y = -(w0_ref[...] + _mm(jnp.tanh(w_lo), w2_ref[...], False))
    softplus = jnp.maximum(y, 0.0) + jnp.log1p(jnp.exp(-jnp.abs(y)))
    logw = -jnp.exp(-softplus - 0.5)
    a = _sigmoid(a0_ref[...] + _mm(a_lo, a2_ref[...], False))
    g = _mm(_sigmoid(g_lo), g2_ref[...], False)

    lane_h = lax.broadcasted_iota(jnp.int32, (1, D_GRP), 1) // HEAD_DIM
    same_head = ((lax.broadcasted_iota(jnp.int32, (D_GRP, D_GRP), 0) // HEAD_DIM)
                 == (lax.broadcasted_iota(jnp.int32, (D_GRP, D_GRP), 1) // HEAD_DIM))
    e_head = same_head.astype(F32)
    kk = k * kk_ref[...]
    kkn = kk * lax.rsqrt(jnp.maximum(_mm(kk * kk, e_head, False), 1e-12))
    k2 = k * (1.0 + (a - 1.0) * ka_ref[...])
    bb = kkn * a
    if valid_len is not None:
        ok = (i * tt + row) < valid_len
        logw = jnp.where(ok, logw, 0.0)
        kkn = jnp.where(ok, kkn, 0.0)
        k2m = jnp.where(ok, k2, 0.0)
        bb = jnp.where(ok, bb, 0.0)
        vm = jnp.where(ok, v, 0.0)
    else:
        k2m, vm = k2, v

    ri = lax.broadcasted_iota(jnp.int32, (ht, ht), 0)
    ci = lax.broadcasted_iota(jnp.int32, (ht, ht), 1)
    same_blk = (ri // chunk) == (ci // chunk)
    strict = same_blk & ((ri % chunk) > (ci % chunk))
    incl = same_blk & ((ri % chunk) >= (ci % chunk))
    eye = (ri == ci).astype(F32)
    tril = (lax.broadcasted_iota(jnp.int32, (chunk, chunk), 0)
            >= lax.broadcasted_iota(jnp.int32, (chunk, chunk), 1)).astype(F32)

    def stack(x):
        return jnp.concatenate([jnp.where(lane_h == h, x, 0.0) for h in range(N_HEADS)], axis=0)

    def tile(x):
        return jnp.concatenate([x] * N_HEADS, axis=0)

    def unstack(x):
        out = jnp.zeros((chunk, D_GRP), F32)
        for h in range(N_HEADS):
            out = jnp.where(lane_h == h, x[h * chunk:(h + 1) * chunk], out)
        return out

    o_parts = []
    for c in range(n_chunks):
        rs = slice(c * chunk, (c + 1) * chunk)
        lw = logw[rs]
        cum = _mm(tril, lw, False)
        cum_t = cum[chunk - 1:chunk]
        inv_p = jnp.exp(-cum)
        to_end = jnp.exp(cum_t - cum)
        kap = kkn[rs] * jnp.exp(cum - lw)
        khat = k2m[rs] * inv_p
        bhat = bb[rs] * inv_p
        rhat = r[rs] * jnp.exp(cum)
        vc = vm[rs]
        xs, rr, ks, bs = stack(kap), stack(rhat), stack(khat), stack(bhat)
        lmat = jnp.where(strict, _mm_nt(xs, bs, bf16), 0.0)
        akk = jnp.where(strict, _mm_nt(xs, ks, bf16), 0.0)
        ark = jnp.where(incl, _mm_nt(rr, ks, bf16), 0.0)
        arb = jnp.where(incl, _mm_nt(rr, bs, bf16), 0.0)
        s_cur = s_ref[...]
        v_t = tile(vc)
        rhs = tile(_mm_nt(kap, s_cur, bf16)) + _mm(akk, v_t, bf16)
        minv = eye - lmat
        lp = lmat
        for _ in range(int(math.log2(chunk)) - 1):
            lp = _mm(lp, lp, bf16)
            minv = _mm(minv, eye + lp, bf16)
        u_t = _mm(minv, rhs, bf16)
        o_t = tile(_mm_nt(rhat, s_cur, bf16)) + _mm(ark, v_t, bf16) - _mm(arb, u_t, bf16)
        u = unstack(u_t)
        o_parts.append(unstack(o_t))
        upd = _mm(vc.T, k2m[rs] * to_end, bf16) - _mm(u.T, bb[rs] * to_end, bf16)
        s_ref[...] = s_cur * jnp.exp(cum_t) + jnp.where(same_head, upd, 0.0)

    o = o_parts[0] if n_chunks == 1 else jnp.concatenate(o_parts, axis=0)
    mean = _mm(o, e_head, False) * (1.0 / HEAD_DIM)
    d = o - mean
    var = _mm(d * d, e_head, False) * (1.0 / HEAD_DIM)
    on = d * lax.rsqrt(var + RWKV_LN_EPS) * lg_ref[...] + lb_ref[...]
    bonus = _mm(r * k2 * rk_ref[...], e_head, False) * v
    o_ref[0] = (on + bonus) * g

    @pl.when(i == pl.num_programs(1) - 1)
    def _():
        s_out_ref[0] = s_ref[...]


def _rwkv(f, shift0, s0_bd, lp, valid_len, bf16):
    b, lpad, fd = f.shape
    tt = _row_tile(lpad, 256)
    chunk = min(64, tt)
    vec = lambda a: a.reshape(1, -1)
    pspec = lambda a: pl.BlockSpec(a.shape, lambda bi, ti: (0,) * a.ndim)
    plist = [vec(lp['mu_shift']), vec(lp['w0']), lp['w2'], vec(lp['a0']), lp['a2'], lp['g2'], vec(lp['k_k']),
             vec(lp['k_a']), vec(lp['r_k']), vec(lp['lnx_g']), vec(lp['lnx_b'])]
    return pl.pallas_call(
        functools.partial(_rwkv_kernel, chunk=chunk, valid_len=None if valid_len == lpad else valid_len, bf16=bf16),
        grid=(b, lpad // tt),
        in_specs=[pl.BlockSpec((1, tt, fd), lambda bi, ti: (bi, ti, 0)),
                  pl.BlockSpec((1, 1, fd), lambda bi, ti: (bi, 0, 0)),
                  pl.BlockSpec((1, D_GRP, D_GRP), lambda bi, ti: (bi, 0, 0))] + [pspec(a) for a in plist],
        out_specs=[pl.BlockSpec((1, tt, D_GRP), lambda bi, ti: (bi, ti, 0)),
                   pl.BlockSpec((1, D_GRP, D_GRP), lambda bi, ti: (bi, 0, 0))],
        out_shape=[jax.ShapeDtypeStruct((b, lpad, D_GRP), F32), jax.ShapeDtypeStruct((b, D_GRP, D_GRP), F32)],
        scratch_shapes=[pltpu.VMEM((D_GRP, D_GRP), F32), pltpu.VMEM((1, fd), F32)],
        compiler_params=_params(("parallel", "arbitrary")),
        name="rwkv",
    )(f, shift0.reshape(b, 1, fd), s0_bd, *plist)


def _s5_kernel(u_ref, h0r_ref, h0i_ref, lre_ref, lim_ref, ldt_ref, bblk_ref, cre_ref, cim_ref, dsk_ref,
               wglu_ref, bglu_ref, y_ref, hr_out_ref, hi_out_ref,
               ar_ref, ai_ref, cr_ref, ci_ref, pwr_ref, pwi_ref, hr_ref, hi_ref, sr_ref, si_ref, perm_ref,
               *, last_row, bf16):
    i = pl.program_id(1)
    tt = u_ref.shape[1]
    seg = tt // 8
    ns = ar_ref.shape[1]

    @pl.when(i == 0)
    def _():
        step = jnp.exp(ldt_ref[...])
        lre, lim = lre_ref[...], lim_ref[...]
        mag = jnp.exp(lre * step)
        are, aim = mag * jnp.cos(lim * step), mag * jnp.sin(lim * step)
        den = lre * lre + lim * lim
        cr_ref[...] = ((are - 1.0) * lre + aim * lim) / den
        ci_ref[...] = (aim * lre - (are - 1.0) * lim) / den
        ar_ref[...] = are
        ai_ref[...] = aim
        pwr_ref[0:8, :] = jnp.broadcast_to(are, (8, ns))
        pwi_ref[0:8, :] = jnp.broadcast_to(aim, (8, ns))
        n = 1
        while n < seg:
            tr, ti = pwr_ref[8 * n - 1:8 * n, :], pwi_ref[8 * n - 1:8 * n, :]
            xr, xi = pwr_ref[0:8 * n, :], pwi_ref[0:8 * n, :]
            pwr_ref[8 * n:16 * n, :] = xr * tr - xi * ti
            pwi_ref[8 * n:16 * n, :] = xr * ti + xi * tr
            n *= 2
        hr_ref[...] = h0r_ref[0]
        hi_ref[...] = h0i_ref[0]
        if seg > 1:
            rr = lax.broadcasted_iota(jnp.int32, (tt, tt), 0)
            cc = lax.broadcasted_iota(jnp.int32, (tt, tt), 1)
            perm_ref[...] = jnp.where((rr % 8) * seg + rr // 8 == cc, 1.0, 0.0).astype(BF16)

    def permute(x, inverse):
        if seg == 1:
            return x
        hi = x.astype(BF16)
        lo = (x - hi.astype(F32)).astype(BF16)
        pm = perm_ref[...]
        dn = (((0,), (0,)), ((), ())) if inverse else (((1,), (0,)), ((), ()))
        return (lax.dot_general(pm, hi, dn, preferred_element_type=F32)
                + lax.dot_general(pm, lo, dn, preferred_element_type=F32))

    u = permute(u_ref[0], False)
    ub = _mm(u, bblk_ref[...], bf16)
    ubr, ubi = ub[:, :ns], ub[:, ns:]
    cr, ci = cr_ref[...], ci_ref[...]
    sr_ref[...] = cr * ubr - ci * ubi
    si_ref[...] = cr * ubi + ci * ubr

    are, aim = ar_ref[...], ai_ref[...]

    def local(j, carry):
        hr, hi = carry
        r0 = pl.multiple_of(j * 8, 8)
        nr = are * hr - aim * hi + sr_ref[pl.ds(r0, 8), :]
        ni = are * hi + aim * hr + si_ref[pl.ds(r0, 8), :]
        sr_ref[pl.ds(r0, 8), :] = nr
        si_ref[pl.ds(r0, 8), :] = ni
        return nr, ni

    z8 = jnp.zeros((8, ns), F32)
    er, ei = lax.fori_loop(0, seg, local, (z8, z8))

    pr_end, pi_end = pwr_ref[tt - 1:tt, :], pwi_ref[tt - 1:tt, :]
    cr_h, ci_h = hr_ref[...], hi_ref[...]
    ent_r, ent_i = [], []
    for s in range(8):
        ent_r.append(cr_h)
        ent_i.append(ci_h)
        cr_h, ci_h = (er[s:s + 1] + pr_end * cr_h - pi_end * ci_h,
                      ei[s:s + 1] + pr_end * ci_h + pi_end * cr_h)
    hr_ref[...] = cr_h
    hi_ref[...] = ci_h
    rep = lambda rows: jnp.broadcast_to(jnp.concatenate(rows, axis=0)[None], (seg, 8, ns)).reshape(tt, ns)
    ent_r, ent_i = rep(ent_r), rep(ent_i)
    pr, pi = pwr_ref[...], pwi_ref[...]
    hre = sr_ref[...] + pr * ent_r - pi * ent_i
    him = si_ref[...] + pr * ent_i + pi * ent_r

    y = _mm(hre, cre_ref[...], bf16) - _mm(him, cim_ref[...], bf16) + dsk_ref[...] * u
    y = 0.5 * y * (1.0 + jnp.tanh(math.sqrt(2.0 / math.pi) * (y + 0.044715 * (y * y * y))))
    y = y * _sigmoid(_mm(y, wglu_ref[...], bf16) + bglu_ref[...])
    y_ref[0] = permute(y, True)

    @pl.when(i == pl.num_programs(1) - 1)
    def _():
        hr_out_ref[0] = hre[last_row:last_row + 1, :]
        hi_out_ref[0] = him[last_row:last_row + 1, :]


def _s5(u, h0r, h0i, lp, valid_len, bf16):
    b, lpad, d = u.shape
    g, n = lp['lam_re'].shape
    p = SSM_GROUP
    ns = g * n
    tt = _row_tile(lpad, 512)
    eye_g = jnp.eye(g, dtype=F32)
    bre = jnp.einsum('gnp,gh->gphn', lp['b_re'], eye_g).reshape(g * p, ns)
    bim = jnp.einsum('gnp,gh->gphn', lp['b_im'], eye_g).reshape(g * p, ns)
    bblk = jnp.concatenate([bre, bim], axis=1)
    cre = jnp.einsum('gpn,gh->gnhp', lp['c_re'], eye_g).reshape(ns, g * p)
    cim = jnp.einsum('gpn,gh->gnhp', lp['c_im'], eye_g).reshape(ns, g * p)
    ldt = jnp.repeat(lp['log_dt'], n).reshape(1, ns)
    plist = [lp['lam_re'].reshape(1, ns), lp['lam_im'].reshape(1, ns), ldt, bblk, cre, cim,
             lp['d_skip'].reshape(1, d), lp['w_glu'], lp['b_glu'].reshape(1, d)]
    pspec = lambda a: pl.BlockSpec(a.shape, lambda bi, ti: (0,) * a.ndim)
    st_spec = pl.BlockSpec((1, 1, ns), lambda bi, ti: (bi, 0, 0))
    seg = tt // 8
    t_last = (valid_len - 1) % tt
    vm = lambda r: pltpu.VMEM((r, ns), F32)
    return pl.pallas_call(
        functools.partial(_s5_kernel, last_row=8 * (t_last % seg) + t_last // seg, bf16=bf16),
        grid=(b, lpad // tt),
        in_specs=[pl.BlockSpec((1, tt, d), lambda bi, ti: (bi, ti, 0)), st_spec, st_spec]
                 + [pspec(a) for a in plist],
        out_specs=[pl.BlockSpec((1, tt, d), lambda bi, ti: (bi, ti, 0)), st_spec, st_spec],
        out_shape=[jax.ShapeDtypeStruct((b, lpad, d), F32), jax.ShapeDtypeStruct((b, 1, ns), F32),
                   jax.ShapeDtypeStruct((b, 1, ns), F32)],
        scratch_shapes=[vm(1), vm(1), vm(1), vm(1), vm(tt), vm(tt), vm(1), vm(1), vm(tt), vm(tt),
                        pltpu.VMEM((tt, tt), BF16)],
        compiler_params=_params(("arbitrary", "arbitrary")),
        name="s5",
    )(u, h0r, h0i, *plist)


def _pool_kernel(u_ref, hist_ref, wp_ref, sc_ref, o_ref, ext_ref, *, pos0):
    i = pl.program_id(1)
    tt = u_ref.shape[1]
    hpad = 16

    @pl.when(i == 0)
    def _():
        ext_ref[0:hpad, :] = hist_ref[0]

    u = u_ref[0]
    ext_ref[hpad:hpad + tt, :] = u
    grp = lax.broadcasted_iota(jnp.int32, (1, D_GRP), 1) // (D_GRP // len(POOL_WINDOWS))
    pos = pos0 + i * tt + lax.broadcasted_iota(jnp.int32, (tt, 1), 0)
    acc = u
    pooled = jnp.zeros_like(u)
    d = 1
    for gi, w in enumerate(POOL_WINDOWS):
        while d < w:
            acc = acc + ext_ref[hpad - d:hpad - d + tt, :]
            d += 1
        cnt = jnp.minimum(pos + 1, w).astype(F32)
        pooled = jnp.where(grp == gi, acc / cnt, pooled)
    pooled = pooled - u
    o_ref[0] = _mm(pooled, wp_ref[...], False) * sc_ref[...]
    ext_ref[0:hpad, :] = ext_ref[tt:tt + hpad, :]


def _pool(u, hist16, lp, pos0):
    b, lpad, d = u.shape
    tt = _row_tile(lpad, 512)
    nw = len(POOL_WINDOWS)
    gw = d // nw
    wp = jnp.einsum('gcd,gh->gchd', lp['w_pool'], jnp.eye(nw, dtype=F32)).reshape(d, d)
    return pl.pallas_call(
        functools.partial(_pool_kernel, pos0=pos0),
        grid=(b, lpad // tt),
        in_specs=[pl.BlockSpec((1, tt, d), lambda bi, ti: (bi, ti, 0)),
                  pl.BlockSpec((1, 16, d), lambda bi, ti: (bi, 0, 0)),
                  pl.BlockSpec((d, d), lambda bi, ti: (0, 0)),
                  pl.BlockSpec((1, d), lambda bi, ti: (0, 0))],
        out_specs=pl.BlockSpec((1, tt, d), lambda bi, ti: (bi, ti, 0)),
        out_shape=jax.ShapeDtypeStruct((b, lpad, d), F32),
        scratch_shapes=[pltpu.VMEM((tt + 16, d), F32)],
        compiler_params=_params(("parallel", "arbitrary")),
        name="pool",
    )(u, hist16, wp, lp['pool_scale'].reshape(1, d))


def _bd_from_heads(s):
    b = s.shape[0]
    return jnp.einsum('bhvk,hg->bhvgk', s, jnp.eye(N_HEADS, dtype=s.dtype)).reshape(b, D_GRP, D_GRP)


def _heads_from_bd(s):
    b = s.shape[0]
    s5 = s.reshape(b, N_HEADS, HEAD_DIM, N_HEADS, HEAD_DIM)
    return jnp.stack([s5[:, h, :, h, :] for h in range(N_HEADS)], axis=1)


def _pad_rows(a, lpad):
    return a if a.shape[1] == lpad else jnp.pad(a, ((0, 0), (0, lpad - a.shape[1]), (0, 0)))


def _run_trunk(x, pos0, cache_k, cache_v, page_table, wkv0, shift0, ssm_re0, ssm_im0, pool0, P, final_g,
               bf16_small):
    bsz, l, d = x.shape
    depth = P['w_in'].shape[0]
    n = bsz * l
    lpad = -(-l // 8) * 8
    fd = shift0.shape[-1]
    widths = (D_GRP, D_GRP, D_GRP, fd, D_GRP, D_GRP)
    decode = cache_k is not None
    if decode:
        page_size = cache_k.shape[2]
        ck = cache_k.reshape(cache_k.shape[0], cache_k.shape[1], page_size, D_GRP)
        cv = cache_v.reshape(ck.shape)
        psum = _page_sums(ck)
    xf = x.reshape(n, d)
    outs = []
    y = None
    for li in range(depth):
        lp = {name: arr[li] for name, arr in P.items()}
        q, k, v, f, us, up = _inproj(xf, lp['norm1_g'], lp['w_in'].astype(BF16), widths)
        if decode:
            top = _select_blocks(page_table, q, psum[li:li + 1], page_size)
            att = _attn_decode(top, page_table, q, k, v, ck, cv, li)
        else:
            att = _attn_prompt(q.reshape(bsz, l, D_GRP), k.reshape(bsz, l, D_GRP),
                               v.reshape(bsz, l, D_GRP)).reshape(n, D_GRP)
        f3 = f.reshape(bsz, l, fd)
        rw, s_new = _rwkv(_pad_rows(f3, lpad), shift0[li], _bd_from_heads(wkv0[li]), lp, l, bf16_small)
        g_n = ssm_re0.shape[2] * ssm_re0.shape[3]
        us3 = us.reshape(bsz, l, D_GRP)
        ss, hr, hi = _s5(_pad_rows(us3, lpad), ssm_re0[li].reshape(bsz, 1, g_n),
                         ssm_im0[li].reshape(bsz, 1, g_n), lp, l, bf16_small)
        up3 = up.reshape(bsz, l, D_GRP)
        hist16 = jnp.pad(pool0[li], ((0, 0), (1, 0), (0, 0)))
        pm = _pool(_pad_rows(up3, lpad), hist16, lp, pos0)
        xf, y = _post(att, rw[:, :l].reshape(n, D_GRP), ss[:, :l].reshape(n, D_GRP),
                      pm[:, :l].reshape(n, D_GRP), xf, lp['mix_g'], lp['w_out'].astype(BF16), lp['norm2_g'],
                      lp['w_gate'].astype(BF16), lp['w_up'].astype(BF16), lp['w_down'].astype(BF16), final_g)
        pool_new = jnp.concatenate([pool0[li], up3], axis=1)[:, -POOL_HIST:]
        outs.append((k.reshape(bsz, l, N_HEADS, HEAD_DIM), v.reshape(bsz, l, N_HEADS, HEAD_DIM),
                     _heads_from_bd(s_new), f3[:, -1], hr.reshape(ssm_re0.shape[1:]),
                     hi.reshape(ssm_im0.shape[1:]), pool_new))
    new_state = [jnp.stack(s, axis=0) for s in zip(*outs)]
    return y.reshape(bsz, l, d), new_state


def kernel(x_prompt, x_sample, cache_k, cache_v, page_table, state_wkv, state_shift, state_ssm_re, state_ssm_im, state_pool, norm1_g, w_in, mu_shift, w0, w2, a0, a2, g2, k_k, k_a, r_k, lnx_g, lnx_b, lam_re, lam_im, log_dt, b_re, b_im, c_re, c_im, d_skip, w_glu, b_glu, w_pool, pool_scale, mix_g, w_out, norm2_g, w_gate, w_up, w_down, final_g):
    P = dict(norm1_g=norm1_g, w_in=w_in, mu_shift=mu_shift, w0=w0, w2=w2, a0=a0, a2=a2, g2=g2,
             k_k=k_k, k_a=k_a, r_k=r_k, lnx_g=lnx_g, lnx_b=lnx_b, lam_re=lam_re, lam_im=lam_im,
             log_dt=log_dt, b_re=b_re, b_im=b_im, c_re=c_re, c_im=c_im, d_skip=d_skip, w_glu=w_glu,
             b_glu=b_glu, w_pool=w_pool, pool_scale=pool_scale, mix_g=mix_g, w_out=w_out,
             norm2_g=norm2_g, w_gate=w_gate, w_up=w_up, w_down=w_down)
    dt = x_prompt.dtype
    bp = x_prompt.shape[0]
    depth = w_in.shape[0]
    z = lambda *s: jnp.zeros((depth, bp) + s, dt)
    y_p, (k_p, v_p, wkv_p, shift_p, sre_p, sim_p, pool_p) = _run_trunk(
        x_prompt, 0, None, None, None, z(*state_wkv.shape[2:]), z(state_shift.shape[2]),
        z(*state_ssm_re.shape[2:]), z(*state_ssm_im.shape[2:]), z(*state_pool.shape[2:]), P, final_g, True)
    past_len = page_table.shape[1] * cache_k.shape[2]
    y_s, (k_s, v_s, wkv_s, shift_s, sre_s, sim_s, pool_s) = _run_trunk(
        x_sample, past_len, cache_k, cache_v, page_table, state_wkv, state_shift, state_ssm_re,
        state_ssm_im, state_pool, P, final_g, False)
    return (y_p, y_s, k_p, v_p, k_s, v_s, wkv_p, wkv_s, shift_p, shift_s,
            sre_p, sim_p, sre_s, sim_s, pool_p, pool_s)
```

```python
import functools
import math

import jax
import jax.numpy as jnp
from jax import lax
from jax.experimental import pallas as pl
from jax.experimental.pallas import tpu as pltpu

F32 = jnp.float32
BF16 = jnp.bfloat16
HI = lax.Precision.HIGHEST

HEAD_DIM = 64
N_HEADS = 4
D_GRP = N_HEADS * HEAD_DIM
MOBA_BLOCK = 256
MOBA_TOPK = 3
RWKV_LN_EPS = 64e-5
RMS_EPS = 1e-6
NEG = -1e30
SSM_GROUP = 16
SSM_STATE = 64
POOL_WINDOWS = (2, 4, 8, 16)
POOL_HIST = 15
ALIBI_SLOPES = tuple(2.0 ** (-8.0 * (h + 1) / N_HEADS) for h in range(N_HEADS))
VMEM_LIMIT = 56 * 1024 * 1024


def _params(sem):
    return pltpu.CompilerParams(dimension_semantics=sem, vmem_limit_bytes=VMEM_LIMIT)


def _mm(a, b, bf16):
    if bf16:
        return jnp.dot(a.astype(BF16), b.astype(BF16), preferred_element_type=F32)
    return jnp.dot(a, b, precision=HI, preferred_element_type=F32)


def _mm_nt(a, b, bf16):
    dn = (((1,), (1,)), ((), ()))
    if bf16:
        return lax.dot_general(a.astype(BF16), b.astype(BF16), dn, preferred_element_type=F32)
    return lax.dot_general(a, b, dn, precision=HI, preferred_element_type=F32)


def _mm_pieces(a, b_exact, n):
    bb = b_exact.astype(BF16)
    out, rest = None, a
    for _ in range(n):
        piece = rest.astype(BF16)
        rest = rest - piece.astype(F32)
        d = jnp.dot(piece, bb, preferred_element_type=F32)
        out = d if out is None else out + d
    return out


def _mm_pieces_l(a_exact, b, n):
    ab = a_exact.astype(BF16)
    out, rest = None, b
    for _ in range(n):
        piece = rest.astype(BF16)
        rest = rest - piece.astype(F32)
        d = jnp.dot(ab, piece, preferred_element_type=F32)
        out = d if out is None else out + d
    return out


def _mm_split(a, b):
    ah, bh = a.astype(BF16), b.astype(BF16)
    al, bl = (a - ah.astype(F32)).astype(BF16), (b - bh.astype(F32)).astype(BF16)
    dot = lambda x, y: jnp.dot(x, y, preferred_element_type=F32)
    return dot(ah, bh) + (dot(ah, bl) + dot(al, bh))


def _rms(x, g):
    return x * lax.rsqrt(jnp.mean(x * x, axis=-1, keepdims=True) + RMS_EPS) * g


def _sigmoid(x):
    return 1.0 / (1.0 + jnp.exp(-x))


def _row_tile(n, target):
    t = min(n, target)
    while n % t:
        t //= 2
    return t


def _inproj_kernel(x_ref, g_ref, w_ref, *out_refs, widths, attn_aux):
    h = _rms(x_ref[...], g_ref[...]).astype(BF16)
    c0 = 0
    zs = []
    for o_ref, wd in zip(out_refs, widths):
        z = jnp.dot(h, w_ref[:, c0:c0 + wd], preferred_element_type=F32)
        o_ref[...] = z
        zs.append(z)
        c0 += wd
    if attn_aux:
        kb_ref, vb_ref, km_ref = out_refs[len(widths):]
        kb_ref[...] = zs[1].astype(BF16)
        vb_ref[...] = zs[2].astype(BF16)
        for j in range(km_ref.shape[1]):
            km_ref[0, j:j + 1, :] = jnp.mean(zs[1][j * MOBA_BLOCK:(j + 1) * MOBA_BLOCK], axis=0, keepdims=True)


def _inproj(x, g, w_bf16, widths, attn_aux):
    n, d = x.shape
    tm = _row_tile(n, 512)
    out_specs = [pl.BlockSpec((tm, wd), lambda i: (i, 0)) for wd in widths]
    out_shape = [jax.ShapeDtypeStruct((n, wd), F32) for wd in widths]
    if attn_aux:
        assert tm % MOBA_BLOCK == 0
        bpt = tm // MOBA_BLOCK
        out_specs += [pl.BlockSpec((tm, D_GRP), lambda i: (i, 0))] * 2 + [pl.BlockSpec((1, bpt, D_GRP), lambda i: (i, 0, 0))]
        out_shape += [jax.ShapeDtypeStruct((n, D_GRP), BF16)] * 2 + [jax.ShapeDtypeStruct((n // tm, bpt, D_GRP), F32)]
    return pl.pallas_call(
        functools.partial(_inproj_kernel, widths=widths, attn_aux=attn_aux),
        grid=(n // tm,),
        in_specs=[pl.BlockSpec((tm, d), lambda i: (i, 0)),
                  pl.BlockSpec((1, d), lambda i: (0, 0)),
                  pl.BlockSpec(w_bf16.shape, lambda i: (0, 0))],
        out_specs=out_specs,
        out_shape=out_shape,
        compiler_params=_params(("parallel",)),
        name="inproj",
    )(x, g.reshape(1, d), w_bf16)


def _post_kernel(att_ref, rw_ref, ss_ref, pm_ref, x_ref, mixg_ref, wout_ref, n2g_ref,
                 wg_ref, wu_ref, wd_ref, fing_ref, *out_refs, final):
    acc = x_ref[...]
    for i, ref in enumerate((att_ref, rw_ref, ss_ref, pm_ref)):
        c = slice(i * D_GRP, (i + 1) * D_GRP)
        a = _rms(ref[...], mixg_ref[:, c]).astype(BF16)
        acc = acc + jnp.dot(a, wout_ref[c, :], preferred_element_type=F32)
    h2 = _rms(acc, n2g_ref[...]).astype(BF16)
    gt = jnp.dot(h2, wg_ref[...], preferred_element_type=F32)
    up = jnp.dot(h2, wu_ref[...], preferred_element_type=F32)
    act = (gt * _sigmoid(gt) * up).astype(BF16)
    acc = acc + jnp.dot(act, wd_ref[...], preferred_element_type=F32)
    out_refs[0][...] = _rms(acc, fing_ref[...]) if final else acc


def _post(att, rw, ss, pm, x, mix_g, wout, n2g, wg, wu, wd, fin_g, final):
    n, d = x.shape
    tm = _row_tile(n, 256)
    row = lambda w: pl.BlockSpec((tm, w), lambda i: (i, 0))
    full = lambda a: pl.BlockSpec(a.shape, lambda i: (0, 0))
    vec = lambda: pl.BlockSpec((1, d), lambda i: (0, 0))
    return pl.pallas_call(
        functools.partial(_post_kernel, final=final),
        grid=(n // tm,),
        in_specs=[row(D_GRP), row(D_GRP), row(D_GRP), row(D_GRP), row(d), vec(), full(wout), vec(),
                  full(wg), full(wu), full(wd), vec()],
        out_specs=[row(d)],
        out_shape=[jax.ShapeDtypeStruct((n, d), F32)],
        compiler_params=_params(("parallel",)),
        name="post",
    )(att, rw, ss, pm, x, mix_g.reshape(1, d), wout, n2g.reshape(1, d), wg, wu, wd, fin_g.reshape(1, d))


def _attn_kernel(q_ref, kb_ref, vb_ref, km_ref, o_ref, *, nblk):
    qi = pl.program_id(1)
    blk = MOBA_BLOCK
    hd2 = 2 * HEAD_DIM
    n_pairs = N_HEADS // 2
    f_row, f_col, f_inv = nblk, nblk + 1, nblk + 2
    lane = lax.broadcasted_iota(jnp.int32, (blk, hd2), 1)
    rowf = lax.broadcasted_iota(jnp.int32, (blk, hd2), 0).astype(F32)
    causal = (lax.broadcasted_iota(jnp.int32, (blk, blk), 0) >= lax.broadcasted_iota(jnp.int32, (blk, blk), 1))
    bi = lax.broadcasted_iota(jnp.int32, (nblk, blk), 0)
    bif = bi.astype(F32)
    km_lane = lax.broadcasted_iota(jnp.int32, (nblk, hd2), 1)
    k_static = jnp.where(lane == f_row, 1.0, jnp.where(lane == f_col, rowf, 0.0))

    def key_feat(j, fully_past):
        return jnp.where(lane == jnp.where(fully_past, j, f_inv), 1.0, k_static).astype(BF16)

    qps = []
    for h in range(N_HEADS):
        cs = slice((h // 2) * hd2, (h // 2 + 1) * hd2)
        q2 = q_ref[0, :, cs]
        slope = ALIBI_SLOPES[h]
        gate = _mm_nt(jnp.where((km_lane // HEAD_DIM) == h % 2, km_ref[0, :, cs], 0.0), q2, False)
        g = jnp.where(bi < qi, gate, NEG)
        sel = bi == qi
        for r in range(MOBA_TOPK):
            m = jnp.max(g, axis=0, keepdims=True)
            idx = jnp.min(jnp.where(g == m, bif, float(nblk)), axis=0, keepdims=True)
            hit = bif == idx
            sel = sel | (hit & (qi > r))
            g = jnp.where(hit, -jnp.inf, g)
        rt = jnp.where(sel, (-slope * blk) * (qi - bi).astype(F32), NEG)
        rt = jnp.concatenate([rt, jnp.zeros((hd2 - nblk, blk), F32)], axis=0).T
        feat = jnp.where(lane == f_row, -slope * rowf,
                         jnp.where(lane == f_col, slope, jnp.where(lane == f_inv, NEG, rt)))
        qm = jnp.where((lane // HEAD_DIM) == h % 2, q2, 0.0) * (HEAD_DIM ** -0.5)
        qps.append(jnp.concatenate([qm.astype(BF16), feat.astype(BF16)], axis=1))

    r_own = pl.multiple_of(qi * blk, blk)
    own_feat = key_feat(qi, True)
    init = []
    for h in range(N_HEADS):
        cs = slice((h // 2) * hd2, (h // 2 + 1) * hd2)
        kp = jnp.concatenate([kb_ref[0, pl.ds(r_own, blk), cs], own_feat], axis=1)
        s = jnp.where(causal, _mm_nt(qps[h], kp, True), NEG)
        m0 = jnp.max(s, axis=-1, keepdims=True)
        p = jnp.exp(s - m0)
        init.append((m0, jnp.sum(p, axis=-1, keepdims=True),
                     jnp.dot(p.astype(BF16), vb_ref[0, pl.ds(r_own, blk), cs], preferred_element_type=F32)))

    def body(t, carry):
        r0 = pl.multiple_of(t * (2 * blk), 2 * blk)
        kfeat = jnp.concatenate([key_feat(2 * t, True), key_feat(2 * t + 1, 2 * t + 1 < qi)], axis=0)
        new = []
        for h in range(N_HEADS):
            cs = slice((h // 2) * hd2, (h // 2 + 1) * hd2)
            kp = jnp.concatenate([kb_ref[0, pl.ds(r0, 2 * blk), cs], kfeat], axis=1)
            m_i, l_i, acc = carry[h]
            s = _mm_nt(qps[h], kp, True)
            m_new = jnp.maximum(m_i, jnp.max(s, axis=-1, keepdims=True))
            p = jnp.exp(s - m_new)
            alpha = jnp.exp(m_i - m_new)
            pv = jnp.dot(p.astype(BF16), vb_ref[0, pl.ds(r0, 2 * blk), cs], preferred_element_type=F32)
            new.append((m_new, alpha * l_i + jnp.sum(p, axis=-1, keepdims=True), alpha * acc + pv))
        return tuple(new)

    res = lax.fori_loop(0, (qi + 1) // 2, body, tuple(init))
    for hp in range(n_pairs):
        o0 = res[2 * hp][2] / res[2 * hp][1]
        o1 = res[2 * hp + 1][2] / res[2 * hp + 1][1]
        o_ref[0, :, hp * hd2:(hp + 1) * hd2] = jnp.where((lane // HEAD_DIM) == 0, o0, o1)


def _attn_prompt(q, kb, vb, kmean):
    b, l, d = q.shape
    nblk = l // MOBA_BLOCK
    assert nblk + 3 <= 2 * HEAD_DIM
    return pl.pallas_call(
        functools.partial(_attn_kernel, nblk=nblk),
        grid=(b, nblk),
        in_specs=[pl.BlockSpec((1, MOBA_BLOCK, d), lambda i, j: (i, j, 0)),
                  pl.BlockSpec((1, l, d), lambda i, j: (i, 0, 0)),
                  pl.BlockSpec((1, l, d), lambda i, j: (i, 0, 0)),
                  pl.BlockSpec((1, nblk, d), lambda i, j: (i, 0, 0))],
        out_specs=pl.BlockSpec((1, MOBA_BLOCK, d), lambda i, j: (i, j, 0)),
        out_shape=jax.ShapeDtypeStruct((b, l, d), F32),
        compiler_params=_params(("parallel", "parallel")),
        name="attn_prompt",
    )(q, kb, vb, kmean)


def _pagesum_kernel(c_ref, o_ref):
    pb = c_ref.shape[1]
    x = c_ref[0].reshape(pb, D_GRP, c_ref.shape[4])
    hi = x.astype(BF16)
    lo = (x - hi.astype(F32)).astype(BF16)
    ones = jnp.ones((8, x.shape[2]), BF16)
    for p in range(pb):
        r = _mm_nt(ones, hi[p], True) + _mm_nt(ones, lo[p], True)
        o_ref[0, p:p + 1, :] = r[0:1]


def _page_sums(cache_t):
    nl, npool, nh, hd, psz = cache_t.shape
    pb = _row_tile(npool, 32)
    return pl.pallas_call(
        _pagesum_kernel,
        grid=(nl, npool // pb),
        in_specs=[pl.BlockSpec((1, pb, nh, hd, psz), lambda l, i: (l, i, 0, 0, 0))],
        out_specs=pl.BlockSpec((1, pb, nh * hd), lambda l, i: (l, i, 0)),
        out_shape=jax.ShapeDtypeStruct((nl, npool, nh * hd), F32),
        compiler_params=_params(("parallel", "parallel")),
        name="page_sums",
    )(cache_t)


def _select_kernel(pt_ref, q_ref, ps_ref, o_ref, kmean_ref, *, nblk, pages_per_blk, inv_blk):
    b = pl.program_id(0)

    def fill(j, _):
        acc = ps_ref[0, pl.ds(pt_ref[b, j * pages_per_blk], 1), :]
        for i in range(1, pages_per_blk):
            acc = acc + ps_ref[0, pl.ds(pt_ref[b, j * pages_per_blk + i], 1), :]
        kmean_ref[pl.ds(j, 1), :] = acc * inv_blk
        return 0

    lax.fori_loop(0, nblk, fill, 0)
    prod = kmean_ref[...] * q_ref[0]
    ind = (lax.broadcasted_iota(jnp.int32, (D_GRP, 128), 0) // HEAD_DIM
           == lax.broadcasted_iota(jnp.int32, (D_GRP, 128), 1)).astype(F32)
    g = _mm(prod, ind, False)
    bi = lax.broadcasted_iota(jnp.int32, (nblk, 128), 0)
    rows = []
    for r in range(MOBA_TOPK):
        m = jnp.max(g, axis=0, keepdims=True)
        idx = jnp.min(jnp.where(g == m, bi, nblk), axis=0, keepdims=True)
        rows.append(idx)
        g = jnp.where(bi == idx, -jnp.inf, g)
    rows.append(jnp.zeros((8 - MOBA_TOPK, 128), jnp.int32))
    o_ref[0] = jnp.concatenate(rows, axis=0)


def _select_blocks(page_table, q, psum_l, page_size):
    bsz, npg = page_table.shape
    pages_per_blk = MOBA_BLOCK // page_size
    nblk = npg // pages_per_blk
    npool = psum_l.shape[1]
    out = pl.pallas_call(
        functools.partial(_select_kernel, nblk=nblk, pages_per_blk=pages_per_blk, inv_blk=1.0 / MOBA_BLOCK),
        grid_spec=pltpu.PrefetchScalarGridSpec(
            num_scalar_prefetch=1,
            grid=(bsz,),
            in_specs=[pl.BlockSpec((1, 1, D_GRP), lambda b, pt: (b, 0, 0)),
                      pl.BlockSpec((1, npool, D_GRP), lambda b, pt: (0, 0, 0))],
            out_specs=pl.BlockSpec((1, 8, 128), lambda b, pt: (b, 0, 0)),
            scratch_shapes=[pltpu.VMEM((nblk, D_GRP), F32)]),
        out_shape=jax.ShapeDtypeStruct((bsz, 8, 128), jnp.int32),
        compiler_params=_params(("arbitrary",)),
        name="select_blocks",
    )(page_table, q.reshape(bsz, 1, D_GRP), psum_l)
    return out[:, :MOBA_TOPK, :N_HEADS]


def _decode_kernel(top_ref, pt_ref, q_ref, kn_ref, vn_ref, *refs, page_size, past_len):
    pages_per_blk = MOBA_BLOCK // page_size
    npg = MOBA_TOPK * pages_per_blk
    k_refs = refs[:N_HEADS * npg]
    v_refs = refs[N_HEADS * npg:2 * N_HEADS * npg]
    o_ref = refs[2 * N_HEADS * npg]
    b = pl.program_id(0)
    lane = lax.broadcasted_iota(jnp.int32, (1, page_size), 1)
    for h in range(N_HEADS):
        rs = slice(h * HEAD_DIM, (h + 1) * HEAD_DIM)
        qc = q_ref[0, rs, :] * (HEAD_DIM ** -0.5)
        s_own = jnp.sum(qc * kn_ref[0, rs, :], axis=0, keepdims=True)
        ss = []
        for i in range(npg):
            blk = top_ref[b, (i // pages_per_blk) * N_HEADS + h]
            dist = (past_len - blk * MOBA_BLOCK - (i % pages_per_blk) * page_size - lane).astype(F32)
            kt = k_refs[h * npg + i][0, 0, 0]
            ss.append(jnp.sum(kt * qc, axis=0, keepdims=True) - ALIBI_SLOPES[h] * dist)
        m = s_own
        for s in ss:
            m = jnp.maximum(m, jnp.max(s, axis=-1, keepdims=True))
        p_own = jnp.exp(s_own - m)
        l = p_own
        acc = jnp.zeros((HEAD_DIM, page_size), F32)
        for i in range(npg):
            p = jnp.exp(ss[i] - m)
            l = l + jnp.sum(p, axis=-1, keepdims=True)
            acc = acc + v_refs[h * npg + i][0, 0, 0] * p
        o_ref[0, rs, :] = (jnp.sum(acc, axis=-1, keepdims=True) + p_own * vn_ref[0, rs, :]) / l


def _attn_decode(top, page_table, q, k_new, v_new, cache_kt, cache_vt, layer):
    bsz, npg_seq = page_table.shape
    page_size = cache_kt.shape[4]
    pages_per_blk = MOBA_BLOCK // page_size
    npg = MOBA_TOPK * pages_per_blk
    past_len = npg_seq * page_size

    def page_spec(h, i):
        def imap(b, top_ref, pt_ref):
            blk = top_ref[b, (i // pages_per_blk) * N_HEADS + h]
            return (layer, pt_ref[b, blk * pages_per_blk + i % pages_per_blk], h, 0, 0)
        return pl.BlockSpec((1, 1, 1, HEAD_DIM, page_size), imap)

    col_spec = pl.BlockSpec((1, D_GRP, 1), lambda b, t, p: (b, 0, 0))
    col = lambda a: a.reshape(bsz, D_GRP, 1)
    page_specs = [page_spec(h, i) for h in range(N_HEADS) for i in range(npg)]
    out = pl.pallas_call(
        functools.partial(_decode_kernel, page_size=page_size, past_len=past_len),
        grid_spec=pltpu.PrefetchScalarGridSpec(
            num_scalar_prefetch=2,
            grid=(bsz,),
            in_specs=[col_spec, col_spec, col_spec] + page_specs * 2,
            out_specs=col_spec),
        out_shape=jax.ShapeDtypeStruct((bsz, D_GRP, 1), F32),
        compiler_params=_params(("arbitrary",)),
        name="attn_decode",
    )(top.reshape(bsz, MOBA_TOPK * N_HEADS), page_table, col(q), col(k_new), col(v_new),
      *([cache_kt] * (N_HEADS * npg)), *([cache_vt] * (N_HEADS * npg)))
    return out.reshape(bsz, D_GRP)


def _rwkv_kernel(f_ref, sh0_ref, s0_ref, mu_ref, w0_ref, w2_ref, a0_ref, a2_ref, g2_ref, kk_ref, ka_ref,
                 rk_ref, lg_ref, lb_ref, o_ref, s_out_ref, s_ref, prev_ref, *, chunk, valid_len, bf16):
    i = pl.program_id(1)
    tt = f_ref.shape[1]
    n_chunks = tt // chunk
    ht = N_HEADS * chunk

    @pl.when(i == 0)
    def _():
        s_ref[...] = s0_ref[0]
        prev_ref[...] = sh0_ref[0]

    f = f_ref[0]
    row = lax.broadcasted_iota(jnp.int32, (tt, 1), 0)
    prev = jnp.where(row == 0, prev_ref[...], pltpu.roll(f, 1, 0))
    prev_ref[...] = f[tt - 1:tt, :]
    m = f + (prev - f) * mu_ref[...]
    r = m[:, 0:D_GRP]
    k = m[:, D_GRP:2 * D_GRP]
    v = m[:, 2 * D_GRP:3 * D_GRP]
    c0 = 3 * D_GRP
    dr = w2_ref.shape[0]
    ar = a2_ref.shape[0]
    w_lo = m[:, c0:c0 + dr]
    a_lo = m[:, c0 + dr:c0 + dr + ar]
    g_lo = m[:, c0 + dr + ar:]
    y = -(w0_ref[...] + _mm_split(jnp.tanh(w_lo), w2_ref[...]))
    softplus = jnp.maximum(y, 0.0) + jnp.log1p(jnp.exp(-jnp.abs(y)))
    logw = -jnp.exp(-softplus - 0.5)
    a = _sigmoid(a0_ref[...] + _mm_split(a_lo, a2_ref[...]))
    g = _mm(_sigmoid(g_lo), g2_ref[...], bf16)

    lane_h = lax.broadcasted_iota(jnp.int32, (1, D_GRP), 1) // HEAD_DIM
    same_head = ((lax.broadcasted_iota(jnp.int32, (D_GRP, D_GRP), 0) // HEAD_DIM)
                 == (lax.broadcasted_iota(jnp.int32, (D_GRP, D_GRP), 1) // HEAD_DIM))
    e_head = same_head.astype(F32)
    kk = k * kk_ref[...]
    kkn = kk * lax.rsqrt(jnp.maximum(_mm_pieces(kk * kk, e_head, 2), 1e-12))
    k2 = k * (1.0 + (a - 1.0) * ka_ref[...])
    bb = kkn * a
    if valid_len is not None:
        ok = (i * tt + row) < valid_len
        logw = jnp.where(ok, logw, 0.0)
        kkn = jnp.where(ok, kkn, 0.0)
        k2m = jnp.where(ok, k2, 0.0)
        bb = jnp.where(ok, bb, 0.0)
        vm = jnp.where(ok, v, 0.0)
    else:
        k2m, vm = k2, v

    ri = lax.broadcasted_iota(jnp.int32, (ht, ht), 0)
    ci = lax.broadcasted_iota(jnp.int32, (ht, ht), 1)
    same_blk = (ri // chunk) == (ci // chunk)
    strict = same_blk & ((ri % chunk) > (ci % chunk))
    incl = same_blk & ((ri % chunk) >= (ci % chunk))
    eye = (ri == ci).astype(F32)
    tril = (lax.broadcasted_iota(jnp.int32, (chunk, chunk), 0)
            >= lax.broadcasted_iota(jnp.int32, (chunk, chunk), 1)).astype(F32)

    def stack(x):
        return jnp.concatenate([jnp.where(lane_h == h, x, 0.0) for h in range(N_HEADS)], axis=0)

    def tile(x):
        return jnp.concatenate([x] * N_HEADS, axis=0)

    def unstack(x):
        out = jnp.zeros((chunk, D_GRP), F32)
        for h in range(N_HEADS):
            out = jnp.where(lane_h == h, x[h * chunk:(h + 1) * chunk], out)
        return out

    cat = lambda parts: parts[0] if len(parts) == 1 else jnp.concatenate(parts, axis=0)
    chunks = [slice(c * chunk, (c + 1) * chunk) for c in range(n_chunks)]

    cum = cat([_mm_pieces_l(tril, logw[rs], 3) for rs in chunks])
    cum_ends = [cum[rs.stop - 1:rs.stop] for rs in chunks]
    cum_end = cat([jnp.broadcast_to(ce, (chunk, D_GRP)) for ce in cum_ends])
    inv_p = jnp.exp(-cum)
    to_end = jnp.exp(cum_end - cum)
    kap = kkn * jnp.exp(cum - logw)
    khat = k2m * inv_p
    bhat = bb * inv_p
    rhat = r * jnp.exp(cum)
    k_end = k2m * to_end
    b_end = bb * to_end
    grams = []
    for rs in chunks:
        xs, rr, ks, bs = stack(kap[rs]), stack(rhat[rs]), stack(khat[rs]), stack(bhat[rs])
        lmat = jnp.where(strict, _mm_nt(xs, bs, bf16), 0.0)
        akk = jnp.where(strict, _mm_nt(xs, ks, bf16), 0.0)
        ark = jnp.where(incl, _mm_nt(rr, ks, bf16), 0.0)
        arb = jnp.where(incl, _mm_nt(rr, bs, bf16), 0.0)
        minv = eye - lmat
        lp = lmat
        for _ in range(int(math.log2(chunk)) - 1):
            lp = _mm(lp, lp, bf16)
            minv = _mm(minv, eye + lp, bf16)
        grams.append((akk, ark, arb, minv))

    o_parts = []
    for rs, ce, (akk, ark, arb, minv) in zip(chunks, cum_ends, grams):
        s_cur = s_ref[...]
        vc = vm[rs]
        v_t = tile(vc)
        u_t = _mm(minv, tile(_mm_nt(kap[rs], s_cur, bf16)) + _mm(akk, v_t, bf16), bf16)
        o_t = tile(_mm_nt(rhat[rs], s_cur, bf16)) + _mm(ark, v_t, bf16) - _mm(arb, u_t, bf16)
        o_parts.append(unstack(o_t))
        upd = _mm(vc.T, k_end[rs], bf16) - _mm(unstack(u_t).T, b_end[rs], bf16)
        s_ref[...] = s_cur * jnp.exp(ce) + jnp.where(same_head, upd, 0.0)

    o = cat(o_parts)
    mean = _mm_pieces(o, e_head, 2) * (1.0 / HEAD_DIM)
    d = o - mean
    var = _mm_pieces(d * d, e_head, 2) * (1.0 / HEAD_DIM)
    on = d * lax.rsqrt(var + RWKV_LN_EPS) * lg_ref[...] + lb_ref[...]
    bonus = _mm_pieces(r * k2 * rk_ref[...], e_head, 2) * v
    o_ref[0] = (on + bonus) * g

    @pl.when(i == pl.num_programs(1) - 1)
    def _():
        s_out_ref[0] = s_ref[...]


def _rwkv(f, shift0, s0_bd, lp, valid_len, bf16):
    b, lpad, fd = f.shape
    tt = _row_tile(lpad, 256)
    chunk = min(64, tt)
    vec = lambda a: a.reshape(1, -1)
    pspec = lambda a: pl.BlockSpec(a.shape, lambda bi, ti: (0,) * a.ndim)
    plist = [vec(lp['mu_shift']), vec(lp['w0']), lp['w2'], vec(lp['a0']), lp['a2'], lp['g2'], vec(lp['k_k']),
             vec(lp['k_a']), vec(lp['r_k']), vec(lp['lnx_g']), vec(lp['lnx_b'])]
    return pl.pallas_call(
        functools.partial(_rwkv_kernel, chunk=chunk, valid_len=None if valid_len == lpad else valid_len, bf16=bf16),
        grid=(b, lpad // tt),
        in_specs=[pl.BlockSpec((1, tt, fd), lambda bi, ti: (bi, ti, 0)),
                  pl.BlockSpec((1, 1, fd), lambda bi, ti: (bi, 0, 0)),
                  pl.BlockSpec((1, D_GRP, D_GRP), lambda bi, ti: (bi, 0, 0))] + [pspec(a) for a in plist],
        out_specs=[pl.BlockSpec((1, tt, D_GRP), lambda bi, ti: (bi, ti, 0)),
                   pl.BlockSpec((1, D_GRP, D_GRP), lambda bi, ti: (bi, 0, 0))],
        out_shape=[jax.ShapeDtypeStruct((b, lpad, D_GRP), F32), jax.ShapeDtypeStruct((b, D_GRP, D_GRP), F32)],
        scratch_shapes=[pltpu.VMEM((D_GRP, D_GRP), F32), pltpu.VMEM((1, fd), F32)],
        compiler_params=_params(("parallel", "arbitrary")),
        name="rwkv",
    )(f, shift0.reshape(b, 1, fd), s0_bd, *plist)


def _s5_kernel(u_ref, h0r_ref, h0i_ref, lre_ref, lim_ref, ldt_ref, bblk_ref, cre_ref, cim_ref, dsk_ref,
               wglu_ref, bglu_ref, y_ref, hr_out_ref, hi_out_ref,
               ar_ref, ai_ref, cr_ref, ci_ref, pwr_ref, pwi_ref, hr_ref, hi_ref, sr_ref, si_ref, perm_ref,
               *, last_row, bf16):
    i = pl.program_id(1)
    tt = u_ref.shape[1]
    seg = tt // 8
    ns = ar_ref.shape[1]

    @pl.when(i == 0)
    def _():
        step = jnp.exp(ldt_ref[...])
        lre, lim = lre_ref[...], lim_ref[...]
        mag = jnp.exp(lre * step)
        are, aim = mag * jnp.cos(lim * step), mag * jnp.sin(lim * step)
        den = lre * lre + lim * lim
        cr_ref[...] = ((are - 1.0) * lre + aim * lim) / den
        ci_ref[...] = (aim * lre - (are - 1.0) * lim) / den
        ar_ref[...] = are
        ai_ref[...] = aim
        pwr_ref[0:8, :] = jnp.broadcast_to(are, (8, ns))
        pwi_ref[0:8, :] = jnp.broadcast_to(aim, (8, ns))
        n = 1
        while n < seg:
            tr, ti = pwr_ref[8 * n - 1:8 * n, :], pwi_ref[8 * n - 1:8 * n, :]
            xr, xi = pwr_ref[0:8 * n, :], pwi_ref[0:8 * n, :]
            pwr_ref[8 * n:16 * n, :] = xr * tr - xi * ti
            pwi_ref[8 * n:16 * n, :] = xr * ti + xi * tr
            n *= 2
        hr_ref[...] = h0r_ref[0]
        hi_ref[...] = h0i_ref[0]
        if seg > 1:
            rr = lax.broadcasted_iota(jnp.int32, (tt, tt), 0)
            cc = lax.broadcasted_iota(jnp.int32, (tt, tt), 1)
            perm_ref[...] = jnp.where((rr % 8) * seg + rr // 8 == cc, 1.0, 0.0).astype(BF16)

    def permute(x, inverse):
        if seg == 1:
            return x
        hi = x.astype(BF16)
        lo = (x - hi.astype(F32)).astype(BF16)
        pm = perm_ref[...]
        dn = (((0,), (0,)), ((), ())) if inverse else (((1,), (0,)), ((), ()))
        return (lax.dot_general(pm, hi, dn, preferred_element_type=F32)
                + lax.dot_general(pm, lo, dn, preferred_element_type=F32))

    u = permute(u_ref[0], False)
    ub = _mm(u, bblk_ref[...], bf16)
    ubr, ubi = ub[:, :ns], ub[:, ns:]
    cr, ci = cr_ref[...], ci_ref[...]
    sr_ref[...] = cr * ubr - ci * ubi
    si_ref[...] = cr * ubi + ci * ubr

    are, aim = ar_ref[...], ai_ref[...]

    def local(j, carry):
        hr, hi = carry
        r0 = pl.multiple_of(j * 8, 8)
        nr = are * hr - aim * hi + sr_ref[pl.ds(r0, 8), :]
        ni = are * hi + aim * hr + si_ref[pl.ds(r0, 8), :]
        sr_ref[pl.ds(r0, 8), :] = nr
        si_ref[pl.ds(r0, 8), :] = ni
        return nr, ni

    z8 = jnp.zeros((8, ns), F32)
    er, ei = lax.fori_loop(0, seg, local, (z8, z8))

    pr_end, pi_end = pwr_ref[tt - 1:tt, :], pwi_ref[tt - 1:tt, :]
    cr_h, ci_h = hr_ref[...], hi_ref[...]
    ent_r, ent_i = [], []
    for s in range(8):
        ent_r.append(cr_h)
        ent_i.append(ci_h)
        cr_h, ci_h = (er[s:s + 1] + pr_end * cr_h - pi_end * ci_h,
                      ei[s:s + 1] + pr_end * ci_h + pi_end * cr_h)
    hr_ref[...] = cr_h
    hi_ref[...] = ci_h
    rep = lambda rows: jnp.broadcast_to(jnp.concatenate(rows, axis=0)[None], (seg, 8, ns)).reshape(tt, ns)
    ent_r, ent_i = rep(ent_r), rep(ent_i)
    pr, pi = pwr_ref[...], pwi_ref[...]
    hre = sr_ref[...] + pr * ent_r - pi * ent_i
    him = si_ref[...] + pr * ent_i + pi * ent_r

    y = _mm(hre, cre_ref[...], bf16) - _mm(him, cim_ref[...], bf16) + dsk_ref[...] * u
    y = 0.5 * y * (1.0 + jnp.tanh(math.sqrt(2.0 / math.pi) * (y + 0.044715 * (y * y * y))))
    y = y * _sigmoid(_mm(y, wglu_ref[...], bf16) + bglu_ref[...])
    y_ref[0] = permute(y, True)

    @pl.when(i == pl.num_programs(1) - 1)
    def _():
        hr_out_ref[0] = hre[last_row:last_row + 1, :]
        hi_out_ref[0] = him[last_row:last_row + 1, :]


def _s5(u, h0r, h0i, lp, valid_len, bf16):
    b, lpad, d = u.shape
    g, n = lp['lam_re'].shape
    p = SSM_GROUP
    ns = g * n
    tt = _row_tile(lpad, 512)
    eye_g = jnp.eye(g, dtype=F32)
    bre = jnp.einsum('gnp,gh->gphn', lp['b_re'], eye_g).reshape(g * p, ns)
    bim = jnp.einsum('gnp,gh->gphn', lp['b_im'], eye_g).reshape(g * p, ns)
    bblk = jnp.concatenate([bre, bim], axis=1)
    cre = jnp.einsum('gpn,gh->gnhp', lp['c_re'], eye_g).reshape(ns, g * p)
    cim = jnp.einsum('gpn,gh->gnhp', lp['c_im'], eye_g).reshape(ns, g * p)
    ldt = jnp.repeat(lp['log_dt'], n).reshape(1, ns)
    plist = [lp['lam_re'].reshape(1, ns), lp['lam_im'].reshape(1, ns), ldt, bblk, cre, cim,
             lp['d_skip'].reshape(1, d), lp['w_glu'], lp['b_glu'].reshape(1, d)]
    pspec = lambda a: pl.BlockSpec(a.shape, lambda bi, ti: (0,) * a.ndim)
    st_spec = pl.BlockSpec((1, 1, ns), lambda bi, ti: (bi, 0, 0))
    seg = tt // 8
    t_last = (valid_len - 1) % tt
    vm = lambda r: pltpu.VMEM((r, ns), F32)
    return pl.pallas_call(
        functools.partial(_s5_kernel, last_row=8 * (t_last % seg) + t_last // seg, bf16=bf16),
        grid=(b, lpad // tt),
        in_specs=[pl.BlockSpec((1, tt, d), lambda bi, ti: (bi, ti, 0)), st_spec, st_spec]
                 + [pspec(a) for a in plist],
        out_specs=[pl.BlockSpec((1, tt, d), lambda bi, ti: (bi, ti, 0)), st_spec, st_spec],
        out_shape=[jax.ShapeDtypeStruct((b, lpad, d), F32), jax.ShapeDtypeStruct((b, 1, ns), F32),
                   jax.ShapeDtypeStruct((b, 1, ns), F32)],
        scratch_shapes=[vm(1), vm(1), vm(1), vm(1), vm(tt), vm(tt), vm(1), vm(1), vm(tt), vm(tt),
                        pltpu.VMEM((tt, tt), BF16)],
        compiler_params=_params(("arbitrary", "arbitrary")),
        name="s5",
    )(u, h0r, h0i, *plist)


def _pool_kernel(u_ref, hist_ref, wp_ref, sc_ref, o_ref, ext_ref, *, pos0):
    i = pl.program_id(1)
    tt = u_ref.shape[1]
    hpad = 16

    @pl.when(i == 0)
    def _():
        ext_ref[0:hpad, :] = hist_ref[0]

    u = u_ref[0]
    ext_ref[hpad:hpad + tt, :] = u
    grp = lax.broadcasted_iota(jnp.int32, (1, D_GRP), 1) // (D_GRP // len(POOL_WINDOWS))
    pos = pos0 + i * tt + lax.broadcasted_iota(jnp.int32, (tt, 1), 0)
    acc = u
    pooled = jnp.zeros_like(u)
    d = 1
    for gi, w in enumerate(POOL_WINDOWS):
        while d < w:
            acc = acc + ext_ref[hpad - d:hpad - d + tt, :]
            d += 1
        cnt = jnp.minimum(pos + 1, w).astype(F32)
        pooled = jnp.where(grp == gi, acc / cnt, pooled)
    pooled = pooled - u
    o_ref[0] = _mm(pooled, wp_ref[...], False) * sc_ref[...]
    ext_ref[0:hpad, :] = ext_ref[tt:tt + hpad, :]


def _pool(u, hist16, lp, pos0):
    b, lpad, d = u.shape
    tt = _row_tile(lpad, 512)
    nw = len(POOL_WINDOWS)
    gw = d // nw
    wp = jnp.einsum('gcd,gh->gchd', lp['w_pool'], jnp.eye(nw, dtype=F32)).reshape(d, d)
    return pl.pallas_call(
        functools.partial(_pool_kernel, pos0=pos0),
        grid=(b, lpad // tt),
        in_specs=[pl.BlockSpec((1, tt, d), lambda bi, ti: (bi, ti, 0)),
                  pl.BlockSpec((1, 16, d), lambda bi, ti: (bi, 0, 0)),
                  pl.BlockSpec((d, d), lambda bi, ti: (0, 0)),
                  pl.BlockSpec((1, d), lambda bi, ti: (0, 0))],
        out_specs=pl.BlockSpec((1, tt, d), lambda bi, ti: (bi, ti, 0)),
        out_shape=jax.ShapeDtypeStruct((b, lpad, d), F32),
        scratch_shapes=[pltpu.VMEM((tt + 16, d), F32)],
        compiler_params=_params(("parallel", "arbitrary")),
        name="pool",
    )(u, hist16, wp, lp['pool_scale'].reshape(1, d))


def _bd_from_heads(s):
    b = s.shape[0]
    return jnp.einsum('bhvk,hg->bhvgk', s, jnp.eye(N_HEADS, dtype=s.dtype)).reshape(b, D_GRP, D_GRP)


def _heads_from_bd(s):
    b = s.shape[0]
    s5 = s.reshape(b, N_HEADS, HEAD_DIM, N_HEADS, HEAD_DIM)
    return jnp.stack([s5[:, h, :, h, :] for h in range(N_HEADS)], axis=1)


def _pad_rows(a, lpad):
    return a if a.shape[1] == lpad else jnp.pad(a, ((0, 0), (0, lpad - a.shape[1]), (0, 0)))


def _run_trunk(x, pos0, cache_k, cache_v, page_table, wkv0, shift0, ssm_re0, ssm_im0, pool0, P, final_g,
               bf16_small):
    bsz, l, d = x.shape
    depth = P['w_in'].shape[0]
    n = bsz * l
    lpad = -(-l // 8) * 8
    fd = shift0.shape[-1]
    widths = (D_GRP, D_GRP, D_GRP, fd, D_GRP, D_GRP)
    decode = cache_k is not None
    if decode:
        page_size = cache_k.shape[2]
        assert (page_table.shape[1] * page_size) % MOBA_BLOCK == 0 and page_table.shape[1] * page_size >= MOBA_TOPK * MOBA_BLOCK
        ck = jnp.transpose(cache_k, (0, 1, 3, 4, 2))
        cv = jnp.transpose(cache_v, (0, 1, 3, 4, 2))
        psum = _page_sums(ck)
    xf = x.reshape(n, d)
    outs = []
    for li in range(depth):
        lp = {name: arr[li] for name, arr in P.items()}
        q, k, v, f, us, up, *aux = _inproj(xf, lp['norm1_g'], lp['w_in'].astype(BF16), widths, not decode)
        if decode:
            top = _select_blocks(page_table, q, psum[li:li + 1], page_size)
            att = _attn_decode(top, page_table, q, k, v, ck, cv, li)
        else:
            kb, vb, km = aux
            att = _attn_prompt(q.reshape(bsz, l, D_GRP), kb.reshape(bsz, l, D_GRP), vb.reshape(bsz, l, D_GRP),
                               km.reshape(bsz, l // MOBA_BLOCK, D_GRP)).reshape(n, D_GRP)
        f3 = f.reshape(bsz, l, fd)
        rw, s_new = _rwkv(_pad_rows(f3, lpad), shift0[li], _bd_from_heads(wkv0[li]), lp, l, bf16_small)
        g_n = ssm_re0.shape[2] * ssm_re0.shape[3]
        us3 = us.reshape(bsz, l, D_GRP)
        ss, hr, hi = _s5(_pad_rows(us3, lpad), ssm_re0[li].reshape(bsz, 1, g_n),
                         ssm_im0[li].reshape(bsz, 1, g_n), lp, l, bf16_small)
        up3 = up.reshape(bsz, l, D_GRP)
        hist16 = jnp.pad(pool0[li], ((0, 0), (1, 0), (0, 0)))
        pm = _pool(_pad_rows(up3, lpad), hist16, lp, pos0)
        xf, = _post(att, rw[:, :l].reshape(n, D_GRP), ss[:, :l].reshape(n, D_GRP),
                    pm[:, :l].reshape(n, D_GRP), xf, lp['mix_g'], lp['w_out'].astype(BF16), lp['norm2_g'],
                    lp['w_gate'].astype(BF16), lp['w_up'].astype(BF16), lp['w_down'].astype(BF16), final_g,
                    li == depth - 1)
        pool_new = jnp.concatenate([pool0[li], up3], axis=1)[:, -POOL_HIST:]
        outs.append((k.reshape(bsz, l, N_HEADS, HEAD_DIM), v.reshape(bsz, l, N_HEADS, HEAD_DIM),
                     _heads_from_bd(s_new), f3[:, -1], hr.reshape(ssm_re0.shape[1:]),
                     hi.reshape(ssm_im0.shape[1:]), pool_new))
    new_state = [jnp.stack(s, axis=0) for s in zip(*outs)]
    return xf.reshape(bsz, l, d), new_state


def kernel(x_prompt, x_sample, cache_k, cache_v, page_table, state_wkv, state_shift, state_ssm_re, state_ssm_im, state_pool, norm1_g, w_in, mu_shift, w0, w2, a0, a2, g2, k_k, k_a, r_k, lnx_g, lnx_b, lam_re, lam_im, log_dt, b_re, b_im, c_re, c_im, d_skip, w_glu, b_glu, w_pool, pool_scale, mix_g, w_out, norm2_g, w_gate, w_up, w_down, final_g):
    P = dict(norm1_g=norm1_g, w_in=w_in, mu_shift=mu_shift, w0=w0, w2=w2, a0=a0, a2=a2, g2=g2,
             k_k=k_k, k_a=k_a, r_k=r_k, lnx_g=lnx_g, lnx_b=lnx_b, lam_re=lam_re, lam_im=lam_im,
             log_dt=log_dt, b_re=b_re, b_im=b_im, c_re=c_re, c_im=c_im, d_skip=d_skip, w_glu=w_glu,
             b_glu=b_glu, w_pool=w_pool, pool_scale=pool_scale, mix_g=mix_g, w_out=w_out,
             norm2_g=norm2_g, w_gate=w_gate, w_up=w_up, w_down=w_down)
    dt = x_prompt.dtype
    bp = x_prompt.shape[0]
    depth = w_in.shape[0]
    z = lambda *s: jnp.zeros((depth, bp) + s, dt)
    y_p, (k_p, v_p, wkv_p, shift_p, sre_p, sim_p, pool_p) = _run_trunk(
        x_prompt, 0, None, None, None, z(*state_wkv.shape[2:]), z(state_shift.shape[2]),
        z(*state_ssm_re.shape[2:]), z(*state_ssm_im.shape[2:]), z(*state_pool.shape[2:]), P, final_g, True)
    past_len = page_table.shape[1] * cache_k.shape[2]
    y_s, (k_s, v_s, wkv_s, shift_s, sre_s, sim_s, pool_s) = _run_trunk(
        x_sample, past_len, cache_k, cache_v, page_table, state_wkv, state_shift, state_ssm_re,
        state_ssm_im, state_pool, P, final_g, False)
    return (y_p, y_s, k_p, v_p, k_s, v_s, wkv_p, wkv_s, shift_p, shift_s,
            sre_p, sim_p, sre_s, sim_s, pool_p, pool_s)
```

```python
import functools
import math

import jax
import jax.numpy as jnp
from jax import lax
from jax.experimental import pallas as pl
from jax.experimental.pallas import tpu as pltpu

F32 = jnp.float32
BF16 = jnp.bfloat16
HI = lax.Precision.HIGHEST

HEAD_DIM = 64
N_HEADS = 4
D_GRP = N_HEADS * HEAD_DIM
MOBA_BLOCK = 256
MOBA_TOPK = 3
RWKV_LN_EPS = 64e-5
RMS_EPS = 1e-6
NEG = -1e30
SSM_GROUP = 16
SSM_STATE = 64
POOL_WINDOWS = (2, 4, 8, 16)
POOL_HIST = 15
ALIBI_SLOPES = tuple(2.0 ** (-8.0 * (h + 1) / N_HEADS) for h in range(N_HEADS))
VMEM_LIMIT = 56 * 1024 * 1024


def _params(sem):
    return pltpu.CompilerParams(dimension_semantics=sem, vmem_limit_bytes=VMEM_LIMIT)


def _mm(a, b, bf16):
    if bf16:
        return jnp.dot(a.astype(BF16), b.astype(BF16), preferred_element_type=F32)
    return jnp.dot(a, b, precision=HI, preferred_element_type=F32)


def _mm_nt(a, b, bf16):
    dn = (((1,), (1,)), ((), ()))
    if bf16:
        return lax.dot_general(a.astype(BF16), b.astype(BF16), dn, preferred_element_type=F32)
    return lax.dot_general(a, b, dn, precision=HI, preferred_element_type=F32)


def _act(x, w_ref):
    return x.astype(BF16) if w_ref.dtype == BF16 else x


def _dot_w(a, w):
    if w.dtype == BF16:
        return jnp.dot(a, w, preferred_element_type=F32)
    return jnp.dot(a, w, precision=HI, preferred_element_type=F32)


def _mm_pieces(a, b_exact, n):
    bb = b_exact.astype(BF16)
    out, rest = None, a
    for _ in range(n):
        piece = rest.astype(BF16)
        rest = rest - piece.astype(F32)
        d = jnp.dot(piece, bb, preferred_element_type=F32)
        out = d if out is None else out + d
    return out


def _mm_pieces_l(a_exact, b, n):
    ab = a_exact.astype(BF16)
    out, rest = None, b
    for _ in range(n):
        piece = rest.astype(BF16)
        rest = rest - piece.astype(F32)
        d = jnp.dot(ab, piece, preferred_element_type=F32)
        out = d if out is None else out + d
    return out


def _mm_split(a, b):
    ah, bh = a.astype(BF16), b.astype(BF16)
    al, bl = (a - ah.astype(F32)).astype(BF16), (b - bh.astype(F32)).astype(BF16)
    dot = lambda x, y: jnp.dot(x, y, preferred_element_type=F32)
    return dot(ah, bh) + (dot(ah, bl) + dot(al, bh))


def _mm_nt_split(a, b):
    dn = (((1,), (1,)), ((), ()))
    ah, bh = a.astype(BF16), b.astype(BF16)
    al, bl = (a - ah.astype(F32)).astype(BF16), (b - bh.astype(F32)).astype(BF16)
    dot = lambda x, y: lax.dot_general(x, y, dn, preferred_element_type=F32)
    return dot(ah, bh) + (dot(ah, bl) + dot(al, bh))


def _rms(x, g):
    return x * lax.rsqrt(jnp.mean(x * x, axis=-1, keepdims=True) + RMS_EPS) * g


def _sigmoid(x):
    return 1.0 / (1.0 + jnp.exp(-x))


def _row_tile(n, target):
    t = min(n, target)
    while n % t:
        t //= 2
    return t


def _inproj_kernel(x_ref, g_ref, w_ref, *out_refs, widths, attn_aux):
    h = _act(_rms(x_ref[...], g_ref[...]), w_ref)
    c0 = 0
    zs = []
    for o_ref, wd in zip(out_refs, widths):
        z = _dot_w(h, w_ref[:, c0:c0 + wd])
        o_ref[...] = z
        zs.append(z)
        c0 += wd
    if attn_aux:
        kb_ref, vb_ref, km_ref = out_refs[len(widths):]
        kb_ref[...] = zs[1].astype(BF16)
        vb_ref[...] = zs[2].astype(BF16)
        for j in range(km_ref.shape[1]):
            km_ref[0, j:j + 1, :] = jnp.mean(zs[1][j * MOBA_BLOCK:(j + 1) * MOBA_BLOCK], axis=0, keepdims=True)


def _inproj(x, g, w, widths, attn_aux):
    n, d = x.shape
    tm = _row_tile(n, 512)
    out_specs = [pl.BlockSpec((tm, wd), lambda i: (i, 0)) for wd in widths]
    out_shape = [jax.ShapeDtypeStruct((n, wd), F32) for wd in widths]
    if attn_aux:
        assert tm % MOBA_BLOCK == 0
        bpt = tm // MOBA_BLOCK
        out_specs += [pl.BlockSpec((tm, D_GRP), lambda i: (i, 0))] * 2 + [pl.BlockSpec((1, bpt, D_GRP), lambda i: (i, 0, 0))]
        out_shape += [jax.ShapeDtypeStruct((n, D_GRP), BF16)] * 2 + [jax.ShapeDtypeStruct((n // tm, bpt, D_GRP), F32)]
    return pl.pallas_call(
        functools.partial(_inproj_kernel, widths=widths, attn_aux=attn_aux),
        grid=(n // tm,),
        in_specs=[pl.BlockSpec((tm, d), lambda i: (i, 0)),
                  pl.BlockSpec((1, d), lambda i: (0, 0)),
                  pl.BlockSpec(w.shape, lambda i: (0, 0), pipeline_mode=pl.Buffered(1))],
        out_specs=out_specs,
        out_shape=out_shape,
        compiler_params=_params(("parallel",)),
        name="inproj",
    )(x, g.reshape(1, d), w)


def _post_kernel(att_ref, rw_ref, ss_ref, pm_ref, x_ref, mixg_ref, wout_ref, n2g_ref,
                 wg_ref, wu_ref, wd_ref, fing_ref, *out_refs, final):
    acc = x_ref[...]
    for i, ref in enumerate((att_ref, rw_ref, ss_ref, pm_ref)):
        c = slice(i * D_GRP, (i + 1) * D_GRP)
        acc = acc + _dot_w(_act(_rms(ref[...], mixg_ref[:, c]), wout_ref), wout_ref[c, :])
    h2 = _act(_rms(acc, n2g_ref[...]), wg_ref)
    gt = _dot_w(h2, wg_ref[...])
    up = _dot_w(h2, wu_ref[...])
    acc = acc + _dot_w(_act(gt * _sigmoid(gt) * up, wd_ref), wd_ref[...])
    out_refs[0][...] = _rms(acc, fing_ref[...]) if final else acc


def _post(att, rw, ss, pm, x, mix_g, wout, n2g, wg, wu, wd, fin_g, final):
    n, d = x.shape
    tm = _row_tile(n, 512)
    row = lambda w: pl.BlockSpec((tm, w), lambda i: (i, 0))
    full = lambda a: pl.BlockSpec(a.shape, lambda i: (0, 0), pipeline_mode=pl.Buffered(1))
    vec = lambda: pl.BlockSpec((1, d), lambda i: (0, 0))
    return pl.pallas_call(
        functools.partial(_post_kernel, final=final),
        grid=(n // tm,),
        in_specs=[row(D_GRP), row(D_GRP), row(D_GRP), row(D_GRP), row(d), vec(), full(wout), vec(),
                  full(wg), full(wu), full(wd), vec()],
        out_specs=[row(d)],
        out_shape=[jax.ShapeDtypeStruct((n, d), F32)],
        compiler_params=_params(("parallel",)),
        name="post",
    )(att, rw, ss, pm, x, mix_g.reshape(1, d), wout, n2g.reshape(1, d), wg, wu, wd, fin_g.reshape(1, d))


def _attn_kernel(q_ref, kb_ref, vb_ref, km_ref, o_ref, *, nblk):
    qi = pl.program_id(1)
    blk = MOBA_BLOCK
    hd2 = 2 * HEAD_DIM
    n_pairs = N_HEADS // 2
    f_row, f_col, f_inv = nblk, nblk + 1, nblk + 2
    lane = lax.broadcasted_iota(jnp.int32, (blk, hd2), 1)
    rowf = lax.broadcasted_iota(jnp.int32, (blk, hd2), 0).astype(F32)
    causal = (lax.broadcasted_iota(jnp.int32, (blk, blk), 0) >= lax.broadcasted_iota(jnp.int32, (blk, blk), 1))
    bi = lax.broadcasted_iota(jnp.int32, (nblk, blk), 0)
    bif = bi.astype(F32)
    km_lane = lax.broadcasted_iota(jnp.int32, (nblk, hd2), 1)
    k_static = jnp.where(lane == f_row, 1.0, jnp.where(lane == f_col, rowf, 0.0))

    def key_feat(j, fully_past):
        return jnp.where(lane == jnp.where(fully_past, j, f_inv), 1.0, k_static).astype(BF16)

    gates = []
    for hp in range(n_pairs):
        cs = slice(hp * hd2, (hp + 1) * hd2)
        km2 = km_ref[0, :, cs]
        km_heads = jnp.concatenate([jnp.where((km_lane // HEAD_DIM) == hh, km2, 0.0) for hh in range(2)], axis=0)
        gates.append(_mm_nt_split(km_heads, q_ref[0, :, cs]))

    qps = []
    for h in range(N_HEADS):
        cs = slice((h // 2) * hd2, (h // 2 + 1) * hd2)
        q2 = q_ref[0, :, cs]
        slope = ALIBI_SLOPES[h]
        gate = gates[h // 2][(h % 2) * nblk:(h % 2 + 1) * nblk]
        g = jnp.where(bi < qi, gate, NEG)
        sel = bi == qi
        for r in range(MOBA_TOPK):
            m = jnp.max(g, axis=0, keepdims=True)
            idx = jnp.min(jnp.where(g == m, bif, float(nblk)), axis=0, keepdims=True)
            hit = bif == idx
            sel = sel | (hit & (qi > r))
            g = jnp.where(hit, -jnp.inf, g)
        rt = jnp.where(sel, (-slope * blk) * (qi - bi).astype(F32), NEG)
        rt = jnp.concatenate([rt, jnp.zeros((hd2 - nblk, blk), F32)], axis=0).T
        feat = jnp.where(lane == f_row, -slope * rowf,
                         jnp.where(lane == f_col, slope, jnp.where(lane == f_inv, NEG, rt)))
        qm = jnp.where((lane // HEAD_DIM) == h % 2, q2, 0.0) * (HEAD_DIM ** -0.5)
        qps.append(jnp.concatenate([qm.astype(BF16), feat.astype(BF16)], axis=1))

    r_own = pl.multiple_of(qi * blk, blk)
    own_feat = key_feat(qi, True)
    kps = [jnp.concatenate([kb_ref[0, pl.ds(r_own, blk), slice(hp * hd2, (hp + 1) * hd2)], own_feat], axis=1)
           for hp in range(n_pairs)]
    ss = [_mm_nt(qps[h], kps[h // 2], True) for h in range(N_HEADS)]
    stats, ps = [], []
    for h in range(N_HEADS):
        s = jnp.where(causal, ss[h], NEG)
        m0 = jnp.max(s, axis=-1, keepdims=True)
        p = jnp.exp(s - m0)
        stats.append((m0, jnp.sum(p, axis=-1, keepdims=True)))
        ps.append(p.astype(BF16))
    init = []
    for h in range(N_HEADS):
        cs = slice((h // 2) * hd2, (h // 2 + 1) * hd2)
        init.append(stats[h] + (jnp.dot(ps[h], vb_ref[0, pl.ds(r_own, blk), cs], preferred_element_type=F32),))

    def body(t, carry):
        r0 = pl.multiple_of(t * (2 * blk), 2 * blk)
        kfeat = jnp.concatenate([key_feat(2 * t, True), key_feat(2 * t + 1, 2 * t + 1 < qi)], axis=0)
        kps = [jnp.concatenate([kb_ref[0, pl.ds(r0, 2 * blk), slice(hp * hd2, (hp + 1) * hd2)], kfeat], axis=1)
               for hp in range(n_pairs)]
        ss = [_mm_nt(qps[h], kps[h // 2], True) for h in range(N_HEADS)]
        stats, ps = [], []
        for h in range(N_HEADS):
            m_i, l_i, _ = carry[h]
            m_new = jnp.maximum(m_i, jnp.max(ss[h], axis=-1, keepdims=True))
            p = jnp.exp(ss[h] - m_new)
            alpha = jnp.exp(m_i - m_new)
            stats.append((m_new, alpha * l_i + jnp.sum(p, axis=-1, keepdims=True), alpha))
            ps.append(p.astype(BF16))
        new = []
        for h in range(N_HEADS):
            cs = slice((h // 2) * hd2, (h // 2 + 1) * hd2)
            pv = jnp.dot(ps[h], vb_ref[0, pl.ds(r0, 2 * blk), cs], preferred_element_type=F32)
            new.append((stats[h][0], stats[h][1], stats[h][2] * carry[h][2] + pv))
        return tuple(new)

    res = lax.fori_loop(0, (qi + 1) // 2, body, tuple(init))
    for hp in range(n_pairs):
        o0 = res[2 * hp][2] / res[2 * hp][1]
        o1 = res[2 * hp + 1][2] / res[2 * hp + 1][1]
        o_ref[0, :, hp * hd2:(hp + 1) * hd2] = jnp.where((lane // HEAD_DIM) == 0, o0, o1)


def _attn_prompt(q, kb, vb, kmean):
    b, l, d = q.shape
    nblk = l // MOBA_BLOCK
    assert nblk + 3 <= 2 * HEAD_DIM
    return pl.pallas_call(
        functools.partial(_attn_kernel, nblk=nblk),
        grid=(b, nblk),
        in_specs=[pl.BlockSpec((1, MOBA_BLOCK, d), lambda i, j: (i, j, 0)),
                  pl.BlockSpec((1, l, d), lambda i, j: (i, 0, 0)),
                  pl.BlockSpec((1, l, d), lambda i, j: (i, 0, 0)),
                  pl.BlockSpec((1, nblk, d), lambda i, j: (i, 0, 0))],
        out_specs=pl.BlockSpec((1, MOBA_BLOCK, d), lambda i, j: (i, j, 0)),
        out_shape=jax.ShapeDtypeStruct((b, l, d), F32),
        compiler_params=_params(("parallel", "parallel")),
        name="attn_prompt",
    )(q, kb, vb, kmean)


def _pagesum_kernel(c_ref, o_ref):
    pb = c_ref.shape[1]
    x = c_ref[0].reshape(pb, D_GRP, c_ref.shape[4])
    hi = x.astype(BF16)
    rest = x - hi.astype(F32)
    mid = rest.astype(BF16)
    lo = (rest - mid.astype(F32)).astype(BF16)
    ones = jnp.ones((8, x.shape[2]), BF16)
    for p in range(pb):
        r = _mm_nt(ones, hi[p], True) + (_mm_nt(ones, mid[p], True) + _mm_nt(ones, lo[p], True))
        o_ref[0, p:p + 1, :] = r[0:1]


def _page_sums(cache_t):
    nl, npool, nh, hd, psz = cache_t.shape
    pb = _row_tile(npool, 32)
    return pl.pallas_call(
        _pagesum_kernel,
        grid=(nl, npool // pb),
        in_specs=[pl.BlockSpec((1, pb, nh, hd, psz), lambda l, i: (l, i, 0, 0, 0))],
        out_specs=pl.BlockSpec((1, pb, nh * hd), lambda l, i: (l, i, 0)),
        out_shape=jax.ShapeDtypeStruct((nl, npool, nh * hd), F32),
        compiler_params=_params(("parallel", "parallel")),
        name="page_sums",
    )(cache_t)


def _select_kernel(pt_ref, q_ref, ps_ref, o_ref, kmean_ref, *, nblk, pages_per_blk, inv_blk):
    b = pl.program_id(0)

    def fill(j, _):
        acc = ps_ref[0, pl.ds(pt_ref[b, j * pages_per_blk], 1), :]
        for i in range(1, pages_per_blk):
            acc = acc + ps_ref[0, pl.ds(pt_ref[b, j * pages_per_blk + i], 1), :]
        kmean_ref[pl.ds(j, 1), :] = acc * inv_blk
        return 0

    lax.fori_loop(0, nblk, fill, 0)
    prod = kmean_ref[...] * q_ref[0]
    ind = (lax.broadcasted_iota(jnp.int32, (D_GRP, 128), 0) // HEAD_DIM
           == lax.broadcasted_iota(jnp.int32, (D_GRP, 128), 1)).astype(F32)
    g = _mm(prod, ind, False)
    bi = lax.broadcasted_iota(jnp.int32, (nblk, 128), 0)
    rows = []
    for r in range(MOBA_TOPK):
        m = jnp.max(g, axis=0, keepdims=True)
        idx = jnp.min(jnp.where(g == m, bi, nblk), axis=0, keepdims=True)
        rows.append(idx)
        g = jnp.where(bi == idx, -jnp.inf, g)
    rows.append(jnp.zeros((8 - MOBA_TOPK, 128), jnp.int32))
    o_ref[0] = jnp.concatenate(rows, axis=0)


def _select_blocks(page_table, q, psum_l, page_size):
    bsz, npg = page_table.shape
    pages_per_blk = MOBA_BLOCK // page_size
    nblk = npg // pages_per_blk
    npool = psum_l.shape[1]
    out = pl.pallas_call(
        functools.partial(_select_kernel, nblk=nblk, pages_per_blk=pages_per_blk, inv_blk=1.0 / MOBA_BLOCK),
        grid_spec=pltpu.PrefetchScalarGridSpec(
            num_scalar_prefetch=1,
            grid=(bsz,),
            in_specs=[pl.BlockSpec((1, 1, D_GRP), lambda b, pt: (b, 0, 0)),
                      pl.BlockSpec((1, npool, D_GRP), lambda b, pt: (0, 0, 0))],
            out_specs=pl.BlockSpec((1, 8, 128), lambda b, pt: (b, 0, 0)),
            scratch_shapes=[pltpu.VMEM((nblk, D_GRP), F32)]),
        out_shape=jax.ShapeDtypeStruct((bsz, 8, 128), jnp.int32),
        compiler_params=_params(("arbitrary",)),
        name="select_blocks",
    )(page_table, q.reshape(bsz, 1, D_GRP), psum_l)
    return out[:, :MOBA_TOPK, :N_HEADS]


def _decode_kernel(top_ref, pt_ref, q_ref, kn_ref, vn_ref, *refs, page_size, past_len):
    pages_per_blk = MOBA_BLOCK // page_size
    npg = MOBA_TOPK * pages_per_blk
    k_refs = refs[:N_HEADS * npg]
    v_refs = refs[N_HEADS * npg:2 * N_HEADS * npg]
    o_ref = refs[2 * N_HEADS * npg]
    b = pl.program_id(0)
    lane = lax.broadcasted_iota(jnp.int32, (1, page_size), 1)
    for h in range(N_HEADS):
        rs = slice(h * HEAD_DIM, (h + 1) * HEAD_DIM)
        qc = q_ref[0, rs, :] * (HEAD_DIM ** -0.5)
        s_own = jnp.sum(qc * kn_ref[0, rs, :], axis=0, keepdims=True)
        ss = []
        for i in range(npg):
            blk = top_ref[b, (i // pages_per_blk) * N_HEADS + h]
            dist = (past_len - blk * MOBA_BLOCK - (i % pages_per_blk) * page_size - lane).astype(F32)
            kt = k_refs[h * npg + i][0, 0, 0]
            ss.append(jnp.sum(kt * qc, axis=0, keepdims=True) - ALIBI_SLOPES[h] * dist)
        m = s_own
        for s in ss:
            m = jnp.maximum(m, jnp.max(s, axis=-1, keepdims=True))
        p_own = jnp.exp(s_own - m)
        l = p_own
        acc = jnp.zeros((HEAD_DIM, page_size), F32)
        for i in range(npg):
            p = jnp.exp(ss[i] - m)
            l = l + jnp.sum(p, axis=-1, keepdims=True)
            acc = acc + v_refs[h * npg + i][0, 0, 0] * p
        o_ref[0, rs, :] = (jnp.sum(acc, axis=-1, keepdims=True) + p_own * vn_ref[0, rs, :]) / l


def _attn_decode(top, page_table, q, k_new, v_new, cache_kt, cache_vt, layer):
    bsz, npg_seq = page_table.shape
    page_size = cache_kt.shape[4]
    pages_per_blk = MOBA_BLOCK // page_size
    npg = MOBA_TOPK * pages_per_blk
    past_len = npg_seq * page_size

    def page_spec(h, i):
        def imap(b, top_ref, pt_ref):
            blk = top_ref[b, (i // pages_per_blk) * N_HEADS + h]
            return (layer, pt_ref[b, blk * pages_per_blk + i % pages_per_blk], h, 0, 0)
        return pl.BlockSpec((1, 1, 1, HEAD_DIM, page_size), imap)

    col_spec = pl.BlockSpec((1, D_GRP, 1), lambda b, t, p: (b, 0, 0))
    col = lambda a: a.reshape(bsz, D_GRP, 1)
    page_specs = [page_spec(h, i) for h in range(N_HEADS) for i in range(npg)]
    out = pl.pallas_call(
        functools.partial(_decode_kernel, page_size=page_size, past_len=past_len),
        grid_spec=pltpu.PrefetchScalarGridSpec(
            num_scalar_prefetch=2,
            grid=(bsz,),
            in_specs=[col_spec, col_spec, col_spec] + page_specs * 2,
            out_specs=col_spec),
        out_shape=jax.ShapeDtypeStruct((bsz, D_GRP, 1), F32),
        compiler_params=_params(("arbitrary",)),
        name="attn_decode",
    )(top.reshape(bsz, MOBA_TOPK * N_HEADS), page_table, col(q), col(k_new), col(v_new),
      *([cache_kt] * (N_HEADS * npg)), *([cache_vt] * (N_HEADS * npg)))
    return out.reshape(bsz, D_GRP)


def _rwkv_kernel(f_ref, sh0_ref, s0_ref, mu_ref, w0_ref, w2_ref, a0_ref, a2_ref, g2_ref, kk_ref, ka_ref,
                 rk_ref, lg_ref, lb_ref, o_ref, s_out_ref, s_ref, prev_ref, *, chunk, valid_len, bf16):
    i = pl.program_id(1)
    nb, tt, fd = f_ref.shape
    n_chunks = tt // chunk
    ht = N_HEADS * chunk
    rows = nb * tt
    n_pc = 2 if bf16 else 3

    @pl.when(i == 0)
    def _():
        s_ref[...] = s0_ref[...]
        prev_ref[...] = sh0_ref[...]

    f = f_ref[...].reshape(rows, fd)
    row = lax.broadcasted_iota(jnp.int32, (rows, 1), 0)
    prev = pltpu.roll(f, 1, 0)
    for b in range(nb):
        prev = jnp.where(row == b * tt, prev_ref[b], prev)
        prev_ref[b] = f[(b + 1) * tt - 1:(b + 1) * tt, :]
    m = f + (prev - f) * mu_ref[...]
    r = m[:, 0:D_GRP]
    k = m[:, D_GRP:2 * D_GRP]
    v = m[:, 2 * D_GRP:3 * D_GRP]
    c0 = 3 * D_GRP
    dr = w2_ref.shape[0]
    ar = a2_ref.shape[0]
    w_lo = m[:, c0:c0 + dr]
    a_lo = m[:, c0 + dr:c0 + dr + ar]
    g_lo = m[:, c0 + dr + ar:]
    y = -(w0_ref[...] + (_mm_split(jnp.tanh(w_lo), w2_ref[...]) if bf16 else _mm(jnp.tanh(w_lo), w2_ref[...], False)))
    softplus = jnp.maximum(y, 0.0) + jnp.log1p(jnp.exp(-jnp.abs(y)))
    logw = -jnp.exp(-softplus - 0.5)
    a = _sigmoid(a0_ref[...] + (_mm_split(a_lo, a2_ref[...]) if bf16 else _mm(a_lo, a2_ref[...], False)))
    g = _mm(_sigmoid(g_lo), g2_ref[...], bf16)

    lane_h = lax.broadcasted_iota(jnp.int32, (1, D_GRP), 1) // HEAD_DIM
    same_head = ((lax.broadcasted_iota(jnp.int32, (D_GRP, D_GRP), 0) // HEAD_DIM)
                 == (lax.broadcasted_iota(jnp.int32, (D_GRP, D_GRP), 1) // HEAD_DIM))
    e_head = same_head.astype(F32)
    kk = k * kk_ref[...]
    kkn = kk * lax.rsqrt(jnp.maximum(_mm_pieces(kk * kk, e_head, n_pc), 1e-12))
    k2 = k * (1.0 + (a - 1.0) * ka_ref[...])
    bb = kkn * a
    if valid_len is not None:
        ok = (i * tt + row % tt) < valid_len
        logw = jnp.where(ok, logw, 0.0)
        kkn = jnp.where(ok, kkn, 0.0)
        k2m = jnp.where(ok, k2, 0.0)
        bb = jnp.where(ok, bb, 0.0)
        vm = jnp.where(ok, v, 0.0)
    else:
        k2m, vm = k2, v

    ri = lax.broadcasted_iota(jnp.int32, (ht, ht), 0)
    ci = lax.broadcasted_iota(jnp.int32, (ht, ht), 1)
    same_blk = (ri // chunk) == (ci // chunk)
    strict = same_blk & ((ri % chunk) > (ci % chunk))
    incl = same_blk & ((ri % chunk) >= (ci % chunk))
    eye = (ri == ci).astype(F32)
    tril = (lax.broadcasted_iota(jnp.int32, (chunk, chunk), 0)
            >= lax.broadcasted_iota(jnp.int32, (chunk, chunk), 1)).astype(F32)

    def stack(x):
        return jnp.concatenate([jnp.where(lane_h == h, x, 0.0) for h in range(N_HEADS)], axis=0)

    def tile(x):
        return jnp.concatenate([x] * N_HEADS, axis=0)

    def unstack(x):
        out = jnp.zeros((chunk, D_GRP), F32)
        for h in range(N_HEADS):
            out = jnp.where(lane_h == h, x[h * chunk:(h + 1) * chunk], out)
        return out

    cat = lambda parts: parts[0] if len(parts) == 1 else jnp.concatenate(parts, axis=0)
    chunks = [[slice(b * tt + c * chunk, b * tt + (c + 1) * chunk) for b in range(nb)] for c in range(n_chunks)]
    flat = [rs for b in range(nb) for rs in (chunks[c][b] for c in range(n_chunks))]

    cum = cat([_mm_pieces_l(tril, logw[rs], 3) for rs in flat])
    cum_end = cat([jnp.broadcast_to(cum[rs.stop - 1:rs.stop], (chunk, D_GRP)) for rs in flat])
    inv_p = jnp.exp(-cum)
    to_end = jnp.exp(cum_end - cum)
    kap = kkn * jnp.exp(cum - logw)
    khat = k2m * inv_p
    bhat = bb * inv_p
    rhat = r * jnp.exp(cum)
    k_end = k2m * to_end
    b_end = bb * to_end
    every = [rs for step in chunks for rs in step]
    stk = {rs.start: (stack(kap[rs]), stack(rhat[rs]), stack(khat[rs]), stack(bhat[rs])) for rs in every}
    lmat = {rs.start: jnp.where(strict, _mm_nt(stk[rs.start][0], stk[rs.start][3], bf16), 0.0) for rs in every}
    akk = {rs.start: jnp.where(strict, _mm_nt(stk[rs.start][0], stk[rs.start][2], bf16), 0.0) for rs in every}
    ark = {rs.start: jnp.where(incl, _mm_nt(stk[rs.start][1], stk[rs.start][2], bf16), 0.0) for rs in every}
    arb = {rs.start: jnp.where(incl, _mm_nt(stk[rs.start][1], stk[rs.start][3], bf16), 0.0) for rs in every}
    v_t = {rs.start: tile(vm[rs]) for rs in every}
    akk_v = {key: _mm(akk[key], v_t[key], bf16) for key in akk}
    ark_v = {key: _mm(ark[key], v_t[key], bf16) for key in ark}
    minv = {key: eye - lmat[key] for key in lmat}
    lp = dict(lmat)
    for _ in range(int(math.log2(chunk)) - 1):
        lp = {key: _mm(lp[key], lp[key], bf16) for key in lp}
        minv = {key: _mm(minv[key], eye + lp[key], bf16) for key in minv}

    state = [s_ref[b] for b in range(nb)]
    o_parts = {}
    for step in chunks:
        keys = [rs.start for rs in step]
        kap_s = [_mm_nt(kap[rs], state[b], bf16) for b, rs in enumerate(step)]
        r_s = [_mm_nt(rhat[rs], state[b], bf16) for b, rs in enumerate(step)]
        u_t = [_mm(minv[key], tile(kap_s[b]) + akk_v[key], bf16) for b, key in enumerate(keys)]
        o_t = [tile(r_s[b]) + ark_v[key] - _mm(arb[key], u_t[b], bf16) for b, key in enumerate(keys)]
        upd = [_mm(vm[rs].T, k_end[rs], bf16) - _mm(unstack(u_t[b]).T, b_end[rs], bf16) for b, rs in enumerate(step)]
        for b, rs in enumerate(step):
            state[b] = state[b] * jnp.exp(cum[rs.stop - 1:rs.stop]) + jnp.where(same_head, upd[b], 0.0)
            o_parts[rs.start] = unstack(o_t[b])
    for b in range(nb):
        s_ref[b] = state[b]

    o = cat([o_parts[rs.start] for rs in flat])
    mean = _mm_pieces(o, e_head, n_pc) * (1.0 / HEAD_DIM)
    d = o - mean
    var = _mm_pieces(d * d, e_head, n_pc) * (1.0 / HEAD_DIM)
    on = d * lax.rsqrt(var + RWKV_LN_EPS) * lg_ref[...] + lb_ref[...]
    bonus = _mm_pieces(r * k2 * rk_ref[...], e_head, n_pc) * v
    o_ref[...] = ((on + bonus) * g).reshape(nb, tt, D_GRP)

    @pl.when(i == pl.num_programs(1) - 1)
    def _():
        s_out_ref[...] = s_ref[...]


def _rwkv(f, shift0, s0_bd, lp, valid_len, bf16):
    b, lpad, fd = f.shape
    tt = _row_tile(lpad, 256)
    chunk = min(64, tt)
    nb = _row_tile(b, max(1, 512 // tt))
    vec = lambda a: a.reshape(1, -1)
    pspec = lambda a: pl.BlockSpec(a.shape, lambda bi, ti: (0,) * a.ndim)
    plist = [vec(lp['mu_shift']), vec(lp['w0']), lp['w2'], vec(lp['a0']), lp['a2'], lp['g2'], vec(lp['k_k']),
             vec(lp['k_a']), vec(lp['r_k']), vec(lp['lnx_g']), vec(lp['lnx_b'])]
    return pl.pallas_call(
        functools.partial(_rwkv_kernel, chunk=chunk, valid_len=None if valid_len == lpad else valid_len, bf16=bf16),
        grid=(b // nb, lpad // tt),
        in_specs=[pl.BlockSpec((nb, tt, fd), lambda bi, ti: (bi, ti, 0)),
                  pl.BlockSpec((nb, 1, fd), lambda bi, ti: (bi, 0, 0)),
                  pl.BlockSpec((nb, D_GRP, D_GRP), lambda bi, ti: (bi, 0, 0))] + [pspec(a) for a in plist],
        out_specs=[pl.BlockSpec((nb, tt, D_GRP), lambda bi, ti: (bi, ti, 0)),
                   pl.BlockSpec((nb, D_GRP, D_GRP), lambda bi, ti: (bi, 0, 0))],
        out_shape=[jax.ShapeDtypeStruct((b, lpad, D_GRP), F32), jax.ShapeDtypeStruct((b, D_GRP, D_GRP), F32)],
        scratch_shapes=[pltpu.VMEM((nb, D_GRP, D_GRP), F32), pltpu.VMEM((nb, 1, fd), F32)],
        compiler_params=_params(("parallel", "arbitrary")),
        name="rwkv",
    )(f, shift0.reshape(b, 1, fd), s0_bd, *plist)


def _s5_discretise(lre, lim, ldt):
    step = jnp.exp(ldt)
    mag = jnp.exp(lre * step)
    are, aim = mag * jnp.cos(lim * step), mag * jnp.sin(lim * step)
    den = lre * lre + lim * lim
    return are, aim, ((are - 1.0) * lre + aim * lim) / den, (aim * lre - (are - 1.0) * lim) / den


def _s5_gate(y, u, dsk, wglu, bglu, bf16):
    y = y + dsk * u
    y = 0.5 * y * (1.0 + jnp.tanh(math.sqrt(2.0 / math.pi) * (y + 0.044715 * (y * y * y))))
    return y * _sigmoid(_mm(y, wglu, bf16) + bglu)


def _s5_step_kernel(u_ref, h0r_ref, h0i_ref, lre_ref, lim_ref, ldt_ref, bblk_ref, cre_ref, cim_ref, dsk_ref,
                    wglu_ref, bglu_ref, y_ref, hr_out_ref, hi_out_ref, *, bf16):
    ns = h0r_ref.shape[1]
    are, aim, cr, ci = _s5_discretise(lre_ref[...], lim_ref[...], ldt_ref[...])
    u = u_ref[...]
    ub = _mm(u, bblk_ref[...], bf16)
    ubr, ubi = ub[:, :ns], ub[:, ns:]
    h0r, h0i = h0r_ref[...], h0i_ref[...]
    hre = are * h0r - aim * h0i + (cr * ubr - ci * ubi)
    him = are * h0i + aim * h0r + (cr * ubi + ci * ubr)
    hr_out_ref[...] = hre
    hi_out_ref[...] = him
    y = _mm(hre, cre_ref[...], bf16) - _mm(him, cim_ref[...], bf16)
    y_ref[...] = _s5_gate(y, u, dsk_ref[...], wglu_ref[...], bglu_ref[...], bf16)


def _s5_kernel(u_ref, h0r_ref, h0i_ref, lre_ref, lim_ref, ldt_ref, bblk_ref, cre_ref, cim_ref, dsk_ref,
               wglu_ref, bglu_ref, y_ref, hr_out_ref, hi_out_ref,
               ar_ref, ai_ref, cr_ref, ci_ref, pwr_ref, pwi_ref, hr_ref, hi_ref, sr_ref, si_ref, perm_ref,
               *, last_row, bf16):
    i = pl.program_id(1)
    tt = u_ref.shape[1]
    seg = tt // 8
    ns = ar_ref.shape[1]

    @pl.when(i == 0)
    def _():
        are, aim, cre0, cim0 = _s5_discretise(lre_ref[...], lim_ref[...], ldt_ref[...])
        cr_ref[...] = cre0
        ci_ref[...] = cim0
        ar_ref[...] = are
        ai_ref[...] = aim
        pwr_ref[0:8, :] = jnp.broadcast_to(are, (8, ns))
        pwi_ref[0:8, :] = jnp.broadcast_to(aim, (8, ns))
        n = 1
        while n < seg:
            tr, ti = pwr_ref[8 * n - 1:8 * n, :], pwi_ref[8 * n - 1:8 * n, :]
            xr, xi = pwr_ref[0:8 * n, :], pwi_ref[0:8 * n, :]
            pwr_ref[8 * n:16 * n, :] = xr * tr - xi * ti
            pwi_ref[8 * n:16 * n, :] = xr * ti + xi * tr
            n *= 2
        hr_ref[...] = h0r_ref[0]
        hi_ref[...] = h0i_ref[0]
        if seg > 1:
            rr = lax.broadcasted_iota(jnp.int32, (tt, tt), 0)
            cc = lax.broadcasted_iota(jnp.int32, (tt, tt), 1)
            perm_ref[...] = jnp.where((rr % 8) * seg + rr // 8 == cc, 1.0, 0.0).astype(BF16)

    def permute(x, inverse):
        if seg == 1:
            return x
        hi = x.astype(BF16)
        lo = (x - hi.astype(F32)).astype(BF16)
        pm = perm_ref[...]
        dn = (((0,), (0,)), ((), ())) if inverse else (((1,), (0,)), ((), ()))
        return (lax.dot_general(pm, hi, dn, preferred_element_type=F32)
                + lax.dot_general(pm, lo, dn, preferred_element_type=F32))

    u = permute(u_ref[0], False)
    ub = _mm(u, bblk_ref[...], bf16)
    ubr, ubi = ub[:, :ns], ub[:, ns:]
    cr, ci = cr_ref[...], ci_ref[...]
    sr_ref[...] = cr * ubr - ci * ubi
    si_ref[...] = cr * ubi + ci * ubr

    are, aim = ar_ref[...], ai_ref[...]

    def local(j, carry):
        hr, hi = carry
        r0 = pl.multiple_of(j * 8, 8)
        nr = are * hr - aim * hi + sr_ref[pl.ds(r0, 8), :]
        ni = are * hi + aim * hr + si_ref[pl.ds(r0, 8), :]
        sr_ref[pl.ds(r0, 8), :] = nr
        si_ref[pl.ds(r0, 8), :] = ni
        return nr, ni

    z8 = jnp.zeros((8, ns), F32)
    er, ei = lax.fori_loop(0, seg, local, (z8, z8))

    pr_end, pi_end = pwr_ref[tt - 1:tt, :], pwi_ref[tt - 1:tt, :]
    cr_h, ci_h = hr_ref[...], hi_ref[...]
    ent_r, ent_i = [], []
    for s in range(8):
        ent_r.append(cr_h)
        ent_i.append(ci_h)
        cr_h, ci_h = (er[s:s + 1] + pr_end * cr_h - pi_end * ci_h,
                      ei[s:s + 1] + pr_end * ci_h + pi_end * cr_h)
    hr_ref[...] = cr_h
    hi_ref[...] = ci_h
    rep = lambda rows: jnp.broadcast_to(jnp.concatenate(rows, axis=0)[None], (seg, 8, ns)).reshape(tt, ns)
    ent_r, ent_i = rep(ent_r), rep(ent_i)
    pr, pi = pwr_ref[...], pwi_ref[...]
    hre = sr_ref[...] + pr * ent_r - pi * ent_i
    him = si_ref[...] + pr * ent_i + pi * ent_r

    y = _mm(hre, cre_ref[...], bf16) - _mm(him, cim_ref[...], bf16)
    y_ref[0] = permute(_s5_gate(y, u, dsk_ref[...], wglu_ref[...], bglu_ref[...], bf16), True)

    @pl.when(i == pl.num_programs(1) - 1)
    def _():
        hr_out_ref[0] = hre[last_row:last_row + 1, :]
        hi_out_ref[0] = him[last_row:last_row + 1, :]


def _s5(u, h0r, h0i, lp, valid_len, bf16):
    b, lpad, d = u.shape
    g, n = lp['lam_re'].shape
    p = SSM_GROUP
    ns = g * n
    tt = _row_tile(lpad, 512)
    eye_g = jnp.eye(g, dtype=F32)
    bre = jnp.einsum('gnp,gh->gphn', lp['b_re'], eye_g).reshape(g * p, ns)
    bim = jnp.einsum('gnp,gh->gphn', lp['b_im'], eye_g).reshape(g * p, ns)
    bblk = jnp.concatenate([bre, bim], axis=1)
    cre = jnp.einsum('gpn,gh->gnhp', lp['c_re'], eye_g).reshape(ns, g * p)
    cim = jnp.einsum('gpn,gh->gnhp', lp['c_im'], eye_g).reshape(ns, g * p)
    ldt = jnp.repeat(lp['log_dt'], n).reshape(1, ns)
    plist = [lp['lam_re'].reshape(1, ns), lp['lam_im'].reshape(1, ns), ldt, bblk, cre, cim,
             lp['d_skip'].reshape(1, d), lp['w_glu'], lp['b_glu'].reshape(1, d)]
    if lpad == 1:
        whole = lambda a: pl.BlockSpec(a.shape, lambda: (0,) * a.ndim)
        args = [u.reshape(b, d), h0r.reshape(b, ns), h0i.reshape(b, ns)] + plist
        y, hr, hi = pl.pallas_call(
            functools.partial(_s5_step_kernel, bf16=bf16),
            in_specs=[whole(a) for a in args],
            out_specs=[whole(jax.ShapeDtypeStruct((b, w), F32)) for w in (d, ns, ns)],
            out_shape=[jax.ShapeDtypeStruct((b, w), F32) for w in (d, ns, ns)],
            compiler_params=pltpu.CompilerParams(vmem_limit_bytes=VMEM_LIMIT),
            name="s5_step",
        )(*args)
        return y.reshape(b, 1, d), hr.reshape(b, 1, ns), hi.reshape(b, 1, ns)
    pspec = lambda a: pl.BlockSpec(a.shape, lambda bi, ti: (0,) * a.ndim)
    st_spec = pl.BlockSpec((1, 1, ns), lambda bi, ti: (bi, 0, 0))
    seg = tt // 8
    t_last = (valid_len - 1) % tt
    vm = lambda r: pltpu.VMEM((r, ns), F32)
    return pl.pallas_call(
        functools.partial(_s5_kernel, last_row=8 * (t_last % seg) + t_last // seg, bf16=bf16),
        grid=(b, lpad // tt),
        in_specs=[pl.BlockSpec((1, tt, d), lambda bi, ti: (bi, ti, 0)), st_spec, st_spec]
                 + [pspec(a) for a in plist],
        out_specs=[pl.BlockSpec((1, tt, d), lambda bi, ti: (bi, ti, 0)), st_spec, st_spec],
        out_shape=[jax.ShapeDtypeStruct((b, lpad, d), F32), jax.ShapeDtypeStruct((b, 1, ns), F32),
                   jax.ShapeDtypeStruct((b, 1, ns), F32)],
        scratch_shapes=[vm(1), vm(1), vm(1), vm(1), vm(tt), vm(tt), vm(1), vm(1), vm(tt), vm(tt),
                        pltpu.VMEM((tt, tt), BF16)],
        compiler_params=_params(("arbitrary", "arbitrary")),
        name="s5",
    )(u, h0r, h0i, *plist)


def _pool_kernel(u_ref, hist_ref, wp_ref, sc_ref, o_ref, ext_ref, *, pos0):
    i = pl.program_id(1)
    tt = u_ref.shape[1]
    hpad = 16

    @pl.when(i == 0)
    def _():
        ext_ref[0:hpad, :] = hist_ref[0]

    u = u_ref[0]
    ext_ref[hpad:hpad + tt, :] = u
    grp = lax.broadcasted_iota(jnp.int32, (1, D_GRP), 1) // (D_GRP // len(POOL_WINDOWS))
    pos = pos0 + i * tt + lax.broadcasted_iota(jnp.int32, (tt, 1), 0)
    acc = u
    pooled = jnp.zeros_like(u)
    d = 1
    for gi, w in enumerate(POOL_WINDOWS):
        while d < w:
            acc = acc + ext_ref[hpad - d:hpad - d + tt, :]
            d += 1
        cnt = jnp.minimum(pos + 1, w).astype(F32)
        pooled = jnp.where(grp == gi, acc / cnt, pooled)
    pooled = pooled - u
    o_ref[0] = _mm(pooled, wp_ref[...], False) * sc_ref[...]
    ext_ref[0:hpad, :] = ext_ref[tt:tt + hpad, :]


def _pool(u, hist16, lp, pos0):
    b, lpad, d = u.shape
    tt = _row_tile(lpad, 512)
    nw = len(POOL_WINDOWS)
    gw = d // nw
    wp = jnp.einsum('gcd,gh->gchd', lp['w_pool'], jnp.eye(nw, dtype=F32)).reshape(d, d)
    return pl.pallas_call(
        functools.partial(_pool_kernel, pos0=pos0),
        grid=(b, lpad // tt),
        in_specs=[pl.BlockSpec((1, tt, d), lambda bi, ti: (bi, ti, 0)),
                  pl.BlockSpec((1, 16, d), lambda bi, ti: (bi, 0, 0)),
                  pl.BlockSpec((d, d), lambda bi, ti: (0, 0)),
                  pl.BlockSpec((1, d), lambda bi, ti: (0, 0))],
        out_specs=pl.BlockSpec((1, tt, d), lambda bi, ti: (bi, ti, 0)),
        out_shape=jax.ShapeDtypeStruct((b, lpad, d), F32),
        scratch_shapes=[pltpu.VMEM((tt + 16, d), F32)],
        compiler_params=_params(("parallel", "arbitrary")),
        name="pool",
    )(u, hist16, wp, lp['pool_scale'].reshape(1, d))


def _bd_from_heads(s):
    b = s.shape[0]
    return jnp.einsum('bhvk,hg->bhvgk', s, jnp.eye(N_HEADS, dtype=s.dtype)).reshape(b, D_GRP, D_GRP)


def _heads_from_bd(s):
    b = s.shape[0]
    s5 = s.reshape(b, N_HEADS, HEAD_DIM, N_HEADS, HEAD_DIM)
    return jnp.stack([s5[:, h, :, h, :] for h in range(N_HEADS)], axis=1)


def _pad_rows(a, lpad):
    return a if a.shape[1] == lpad else jnp.pad(a, ((0, 0), (0, lpad - a.shape[1]), (0, 0)))


def _run_trunk(x, pos0, cache_k, cache_v, page_table, wkv0, shift0, ssm_re0, ssm_im0, pool0, P, final_g,
               bf16_small):
    bsz, l, d = x.shape
    depth = P['w_in'].shape[0]
    n = bsz * l
    lpad = -(-l // 8) * 8
    fd = shift0.shape[-1]
    widths = (D_GRP, D_GRP, D_GRP, fd, D_GRP, D_GRP)
    decode = cache_k is not None
    wdt = F32 if decode else BF16
    if decode:
        page_size = cache_k.shape[2]
        assert (page_table.shape[1] * page_size) % MOBA_BLOCK == 0 and page_table.shape[1] * page_size >= MOBA_TOPK * MOBA_BLOCK
        ck = jnp.transpose(cache_k, (0, 1, 3, 4, 2))
        cv = jnp.transpose(cache_v, (0, 1, 3, 4, 2))
        psum = _page_sums(ck)
    xf = x.reshape(n, d)
    outs = []
    for li in range(depth):
        lp = {name: arr[li] for name, arr in P.items()}
        q, k, v, f, us, up, *aux = _inproj(xf, lp['norm1_g'], lp['w_in'].astype(wdt), widths, not decode)
        if decode:
            top = _select_blocks(page_table, q, psum[li:li + 1], page_size)
            att = _attn_decode(top, page_table, q, k, v, ck, cv, li)
        else:
            kb, vb, km = aux
            att = _attn_prompt(q.reshape(bsz, l, D_GRP), kb.reshape(bsz, l, D_GRP), vb.reshape(bsz, l, D_GRP),
                               km.reshape(bsz, l // MOBA_BLOCK, D_GRP)).reshape(n, D_GRP)
        f3 = f.reshape(bsz, l, fd)
        rw, s_new = _rwkv(_pad_rows(f3, lpad), shift0[li], _bd_from_heads(wkv0[li]), lp, l, bf16_small)
        g_n = ssm_re0.shape[2] * ssm_re0.shape[3]
        us3 = us.reshape(bsz, l, D_GRP)
        ss, hr, hi = _s5(us3 if l == 1 else _pad_rows(us3, lpad), ssm_re0[li].reshape(bsz, 1, g_n),
                         ssm_im0[li].reshape(bsz, 1, g_n), lp, l, bf16_small)
        up3 = up.reshape(bsz, l, D_GRP)
        hist16 = jnp.pad(pool0[li], ((0, 0), (1, 0), (0, 0)))
        pm = _pool(_pad_rows(up3, lpad), hist16, lp, pos0)
        xf, = _post(att, rw[:, :l].reshape(n, D_GRP), ss[:, :l].reshape(n, D_GRP),
                    pm[:, :l].reshape(n, D_GRP), xf, lp['mix_g'], lp['w_out'].astype(wdt), lp['norm2_g'],
                    lp['w_gate'].astype(wdt), lp['w_up'].astype(wdt), lp['w_down'].astype(wdt), final_g,
                    li == depth - 1)
        pool_new = jnp.concatenate([pool0[li], up3], axis=1)[:, -POOL_HIST:]
        outs.append((k.reshape(bsz, l, N_HEADS, HEAD_DIM), v.reshape(bsz, l, N_HEADS, HEAD_DIM),
                     _heads_from_bd(s_new), f3[:, -1], hr.reshape(ssm_re0.shape[1:]),
                     hi.reshape(ssm_im0.shape[1:]), pool_new))
    new_state = [jnp.stack(s, axis=0) for s in zip(*outs)]
    return xf.reshape(bsz, l, d), new_state


def kernel(x_prompt, x_sample, cache_k, cache_v, page_table, state_wkv, state_shift, state_ssm_re, state_ssm_im, state_pool, norm1_g, w_in, mu_shift, w0, w2, a0, a2, g2, k_k, k_a, r_k, lnx_g, lnx_b, lam_re, lam_im, log_dt, b_re, b_im, c_re, c_im, d_skip, w_glu, b_glu, w_pool, pool_scale, mix_g, w_out, norm2_g, w_gate, w_up, w_down, final_g):
    P = dict(norm1_g=norm1_g, w_in=w_in, mu_shift=mu_shift, w0=w0, w2=w2, a0=a0, a2=a2, g2=g2,
             k_k=k_k, k_a=k_a, r_k=r_k, lnx_g=lnx_g, lnx_b=lnx_b, lam_re=lam_re, lam_im=lam_im,
             log_dt=log_dt, b_re=b_re, b_im=b_im, c_re=c_re, c_im=c_im, d_skip=d_skip, w_glu=w_glu,
             b_glu=b_glu, w_pool=w_pool, pool_scale=pool_scale, mix_g=mix_g, w_out=w_out,
             norm2_g=norm2_g, w_gate=w_gate, w_up=w_up, w_down=w_down)
    dt = x_prompt.dtype
    bp = x_prompt.shape[0]
    depth = w_in.shape[0]
    z = lambda *s: jnp.zeros((depth, bp) + s, dt)
    y_p, (k_p, v_p, wkv_p, shift_p, sre_p, sim_p, pool_p) = _run_trunk(
        x_prompt, 0, None, None, None, z(*state_wkv.shape[2:]), z(state_shift.shape[2]),
        z(*state_ssm_re.shape[2:]), z(*state_ssm_im.shape[2:]), z(*state_pool.shape[2:]), P, final_g, True)
    past_len = page_table.shape[1] * cache_k.shape[2]
    y_s, (k_s, v_s, wkv_s, shift_s, sre_s, sim_s, pool_s) = _run_trunk(
        x_sample, past_len, cache_k, cache_v, page_table, state_wkv, state_shift, state_ssm_re,
        state_ssm_im, state_pool, P, final_g, False)
    return (y_p, y_s, k_p, v_p, k_s, v_s, wkv_p, wkv_s, shift_p, shift_s,
            sre_p, sim_p, sre_s, sim_s, pool_p, pool_s)
```

```python
import functools
import math

import jax
import jax.numpy as jnp
from jax import lax
from jax.experimental import pallas as pl
from jax.experimental.pallas import tpu as pltpu

F32 = jnp.float32
BF16 = jnp.bfloat16
HI = lax.Precision.HIGHEST

HEAD_DIM = 64
N_HEADS = 4
D_GRP = N_HEADS * HEAD_DIM
MOBA_BLOCK = 256
MOBA_TOPK = 3
RWKV_LN_EPS = 64e-5
RMS_EPS = 1e-6
NEG = -1e30
SSM_GROUP = 16
SSM_STATE = 64
POOL_WINDOWS = (2, 4, 8, 16)
POOL_HIST = 15
ALIBI_SLOPES = tuple(2.0 ** (-8.0 * (h + 1) / N_HEADS) for h in range(N_HEADS))
VMEM_LIMIT = 56 * 1024 * 1024


def _params(sem):
    return pltpu.CompilerParams(dimension_semantics=sem, vmem_limit_bytes=VMEM_LIMIT)


def _mm(a, b, bf16):
    if bf16:
        return jnp.dot(a.astype(BF16), b.astype(BF16), preferred_element_type=F32)
    return jnp.dot(a, b, precision=HI, preferred_element_type=F32)


def _mm_nt(a, b, bf16):
    dn = (((1,), (1,)), ((), ()))
    if bf16:
        return lax.dot_general(a.astype(BF16), b.astype(BF16), dn, preferred_element_type=F32)
    return lax.dot_general(a, b, dn, precision=HI, preferred_element_type=F32)


def _act(x, w_ref):
    return x.astype(BF16) if w_ref.dtype == BF16 else x


def _dot_w(a, w):
    if w.dtype == BF16:
        return jnp.dot(a, w, preferred_element_type=F32)
    return jnp.dot(a, w, precision=HI, preferred_element_type=F32)


def _mm_pieces(a, b_exact, n):
    bb = b_exact.astype(BF16)
    out, rest = None, a
    for _ in range(n):
        piece = rest.astype(BF16)
        rest = rest - piece.astype(F32)
        d = jnp.dot(piece, bb, preferred_element_type=F32)
        out = d if out is None else out + d
    return out


def _mm_pieces_l(a_exact, b, n):
    ab = a_exact.astype(BF16)
    out, rest = None, b
    for _ in range(n):
        piece = rest.astype(BF16)
        rest = rest - piece.astype(F32)
        d = jnp.dot(ab, piece, preferred_element_type=F32)
        out = d if out is None else out + d
    return out


def _mm_split(a, b):
    ah, bh = a.astype(BF16), b.astype(BF16)
    al, bl = (a - ah.astype(F32)).astype(BF16), (b - bh.astype(F32)).astype(BF16)
    dot = lambda x, y: jnp.dot(x, y, preferred_element_type=F32)
    return dot(ah, bh) + (dot(ah, bl) + dot(al, bh))


def _mm_nt_split(a, b):
    dn = (((1,), (1,)), ((), ()))
    ah, bh = a.astype(BF16), b.astype(BF16)
    al, bl = (a - ah.astype(F32)).astype(BF16), (b - bh.astype(F32)).astype(BF16)
    dot = lambda x, y: lax.dot_general(x, y, dn, preferred_element_type=F32)
    return dot(ah, bh) + (dot(ah, bl) + dot(al, bh))


def _rms(x, g):
    return x * lax.rsqrt(jnp.mean(x * x, axis=-1, keepdims=True) + RMS_EPS) * g


def _sigmoid(x):
    return 1.0 / (1.0 + jnp.exp(-x))


def _row_tile(n, target):
    t = min(n, target)
    while n % t:
        t //= 2
    return t


def _inproj_kernel(x_ref, g_ref, w_ref, *out_refs, widths, attn_aux):
    h = _act(_rms(x_ref[...], g_ref[0]), w_ref)
    c0 = 0
    zs = []
    for o_ref, wd in zip(out_refs, widths):
        z = _dot_w(h, w_ref[0, :, c0:c0 + wd])
        o_ref[...] = z
        zs.append(z)
        c0 += wd
    if attn_aux:
        kb_ref, vb_ref, km_ref = out_refs[len(widths):]
        kb_ref[...] = zs[1].astype(BF16)
        vb_ref[...] = zs[2].astype(BF16)
        for j in range(km_ref.shape[1]):
            km_ref[0, j:j + 1, :] = jnp.mean(zs[1][j * MOBA_BLOCK:(j + 1) * MOBA_BLOCK], axis=0, keepdims=True)


def _inproj(x, g, w, layer, widths, attn_aux):
    n, d = x.shape
    tm = _row_tile(n, 512)
    out_specs = [pl.BlockSpec((tm, wd), lambda i: (i, 0)) for wd in widths]
    out_shape = [jax.ShapeDtypeStruct((n, wd), F32) for wd in widths]
    if attn_aux:
        assert tm % MOBA_BLOCK == 0
        bpt = tm // MOBA_BLOCK
        out_specs += [pl.BlockSpec((tm, D_GRP), lambda i: (i, 0))] * 2 + [pl.BlockSpec((1, bpt, D_GRP), lambda i: (i, 0, 0))]
        out_shape += [jax.ShapeDtypeStruct((n, D_GRP), BF16)] * 2 + [jax.ShapeDtypeStruct((n // tm, bpt, D_GRP), F32)]
    return pl.pallas_call(
        functools.partial(_inproj_kernel, widths=widths, attn_aux=attn_aux),
        grid=(n // tm,),
        in_specs=[pl.BlockSpec((tm, d), lambda i: (i, 0)),
                  pl.BlockSpec((1, 1, d), lambda i: (layer, 0, 0)),
                  pl.BlockSpec((1,) + w.shape[1:], lambda i: (layer, 0, 0), pipeline_mode=pl.Buffered(1))],
        out_specs=out_specs,
        out_shape=out_shape,
        compiler_params=_params(("parallel",)),
        name="inproj",
    )(x, g.reshape(g.shape[0], 1, d), w)


def _post_kernel(att_ref, rw_ref, ss_ref, pm_ref, x_ref, mixg_ref, wout_ref, n2g_ref,
                 wg_ref, wu_ref, wd_ref, fing_ref, *out_refs, final):
    acc = x_ref[...]
    for i, ref in enumerate((att_ref, rw_ref, ss_ref, pm_ref)):
        c = slice(i * D_GRP, (i + 1) * D_GRP)
        acc = acc + _dot_w(_act(_rms(ref[...], mixg_ref[0, :, c]), wout_ref), wout_ref[0, c, :])
    h2 = _act(_rms(acc, n2g_ref[0]), wg_ref)
    gt = _dot_w(h2, wg_ref[0])
    up = _dot_w(h2, wu_ref[0])
    acc = acc + _dot_w(_act(gt * _sigmoid(gt) * up, wd_ref), wd_ref[0])
    out_refs[0][...] = _rms(acc, fing_ref[...]) if final else acc


def _post(att, rw, ss, pm, x, mix_g, wout, n2g, wg, wu, wd, fin_g, layer, final):
    n, d = x.shape
    tm = _row_tile(n, 512)
    row = lambda w: pl.BlockSpec((tm, w), lambda i: (i, 0))
    full = lambda a: pl.BlockSpec((1,) + a.shape[1:], lambda i: (layer, 0, 0), pipeline_mode=pl.Buffered(1))
    vec = lambda: pl.BlockSpec((1, 1, d), lambda i: (layer, 0, 0))
    v3 = lambda a: a.reshape(a.shape[0], 1, d)
    return pl.pallas_call(
        functools.partial(_post_kernel, final=final),
        grid=(n // tm,),
        in_specs=[row(D_GRP), row(D_GRP), row(D_GRP), row(D_GRP), row(d), vec(), full(wout), vec(),
                  full(wg), full(wu), full(wd), pl.BlockSpec((1, d), lambda i: (0, 0))],
        out_specs=[row(d)],
        out_shape=[jax.ShapeDtypeStruct((n, d), F32)],
        compiler_params=_params(("parallel",)),
        name="post",
    )(att, rw, ss, pm, x, v3(mix_g), wout, v3(n2g), wg, wu, wd, fin_g.reshape(1, d))


def _attn_kernel(q_ref, kb_ref, vb_ref, km_ref, o_ref, *, nblk):
    qi = pl.program_id(1)
    blk = MOBA_BLOCK
    hd2 = 2 * HEAD_DIM
    n_pairs = N_HEADS // 2
    f_row, f_col, f_inv = nblk, nblk + 1, nblk + 2
    lane = lax.broadcasted_iota(jnp.int32, (blk, hd2), 1)
    rowf = lax.broadcasted_iota(jnp.int32, (blk, hd2), 0).astype(F32)
    causal = (lax.broadcasted_iota(jnp.int32, (blk, blk), 0) >= lax.broadcasted_iota(jnp.int32, (blk, blk), 1))
    bi = lax.broadcasted_iota(jnp.int32, (nblk, blk), 0)
    bif = bi.astype(F32)
    km_lane = lax.broadcasted_iota(jnp.int32, (nblk, hd2), 1)
    k_static = jnp.where(lane == f_row, 1.0, jnp.where(lane == f_col, rowf, 0.0))

    def key_feat(j, fully_past):
        return jnp.where(lane == jnp.where(fully_past, j, f_inv), 1.0, k_static).astype(BF16)

    gates = []
    for hp in range(n_pairs):
        cs = slice(hp * hd2, (hp + 1) * hd2)
        km2 = km_ref[0, :, cs]
        km_heads = jnp.concatenate([jnp.where((km_lane // HEAD_DIM) == hh, km2, 0.0) for hh in range(2)], axis=0)
        gates.append(_mm_nt_split(km_heads, q_ref[0, :, cs]))

    qps = []
    for h in range(N_HEADS):
        cs = slice((h // 2) * hd2, (h // 2 + 1) * hd2)
        q2 = q_ref[0, :, cs]
        slope = ALIBI_SLOPES[h]
        gate = gates[h // 2][(h % 2) * nblk:(h % 2 + 1) * nblk]
        g = jnp.where(bi < qi, gate, NEG)
        sel = bi == qi
        for r in range(MOBA_TOPK):
            m = jnp.max(g, axis=0, keepdims=True)
            idx = jnp.min(jnp.where(g == m, bif, float(nblk)), axis=0, keepdims=True)
            hit = bif == idx
            sel = sel | (hit & (qi > r))
            g = jnp.where(hit, -jnp.inf, g)
        rt = jnp.where(sel, (-slope * blk) * (qi - bi).astype(F32), NEG)
        rt = jnp.concatenate([rt, jnp.zeros((hd2 - nblk, blk), F32)], axis=0).T
        feat = jnp.where(lane == f_row, -slope * rowf,
                         jnp.where(lane == f_col, slope, jnp.where(lane == f_inv, NEG, rt)))
        qm = jnp.where((lane // HEAD_DIM) == h % 2, q2, 0.0) * (HEAD_DIM ** -0.5)
        qps.append(jnp.concatenate([qm.astype(BF16), feat.astype(BF16)], axis=1))

    r_own = pl.multiple_of(qi * blk, blk)
    own_feat = key_feat(qi, True)
    kps = [jnp.concatenate([kb_ref[0, pl.ds(r_own, blk), slice(hp * hd2, (hp + 1) * hd2)], own_feat], axis=1)
           for hp in range(n_pairs)]
    ss = [_mm_nt(qps[h], kps[h // 2], True) for h in range(N_HEADS)]
    stats, ps = [], []
    for h in range(N_HEADS):
        s = jnp.where(causal, ss[h], NEG)
        m0 = jnp.max(s, axis=-1, keepdims=True)
        p = jnp.exp(s - m0)
        stats.append((m0, jnp.sum(p, axis=-1, keepdims=True)))
        ps.append(p.astype(BF16))
    init = []
    for h in range(N_HEADS):
        cs = slice((h // 2) * hd2, (h // 2 + 1) * hd2)
        init.append(stats[h] + (jnp.dot(ps[h], vb_ref[0, pl.ds(r_own, blk), cs], preferred_element_type=F32),))

    def body(t, carry):
        r0 = pl.multiple_of(t * (2 * blk), 2 * blk)
        kfeat = jnp.concatenate([key_feat(2 * t, True), key_feat(2 * t + 1, 2 * t + 1 < qi)], axis=0)
        kps = [jnp.concatenate([kb_ref[0, pl.ds(r0, 2 * blk), slice(hp * hd2, (hp + 1) * hd2)], kfeat], axis=1)
               for hp in range(n_pairs)]
        ss = [_mm_nt(qps[h], kps[h // 2], True) for h in range(N_HEADS)]
        stats, ps = [], []
        for h in range(N_HEADS):
            m_i, l_i, _ = carry[h]
            m_new = jnp.maximum(m_i, jnp.max(ss[h], axis=-1, keepdims=True))
            p = jnp.exp(ss[h] - m_new)
            alpha = jnp.exp(m_i - m_new)
            stats.append((m_new, alpha * l_i + jnp.sum(p, axis=-1, keepdims=True), alpha))
            ps.append(p.astype(BF16))
        new = []
        for h in range(N_HEADS):
            cs = slice((h // 2) * hd2, (h // 2 + 1) * hd2)
            pv = jnp.dot(ps[h], vb_ref[0, pl.ds(r0, 2 * blk), cs], preferred_element_type=F32)
            new.append((stats[h][0], stats[h][1], stats[h][2] * carry[h][2] + pv))
        return tuple(new)

    res = lax.fori_loop(0, (qi + 1) // 2, body, tuple(init))
    for hp in range(n_pairs):
        o0 = res[2 * hp][2] / res[2 * hp][1]
        o1 = res[2 * hp + 1][2] / res[2 * hp + 1][1]
        o_ref[0, :, hp * hd2:(hp + 1) * hd2] = jnp.where((lane // HEAD_DIM) == 0, o0, o1)


def _attn_prompt(q, kb, vb, kmean):
    b, l, d = q.shape
    nblk = l // MOBA_BLOCK
    assert nblk + 3 <= 2 * HEAD_DIM
    return pl.pallas_call(
        functools.partial(_attn_kernel, nblk=nblk),
        grid=(b, nblk),
        in_specs=[pl.BlockSpec((1, MOBA_BLOCK, d), lambda i, j: (i, j, 0)),
                  pl.BlockSpec((1, l, d), lambda i, j: (i, 0, 0)),
                  pl.BlockSpec((1, l, d), lambda i, j: (i, 0, 0)),
                  pl.BlockSpec((1, nblk, d), lambda i, j: (i, 0, 0))],
        out_specs=pl.BlockSpec((1, MOBA_BLOCK, d), lambda i, j: (i, j, 0)),
        out_shape=jax.ShapeDtypeStruct((b, l, d), F32),
        compiler_params=_params(("parallel", "parallel")),
        name="attn_prompt",
    )(q, kb, vb, kmean)


def _gate_select_kernel(pt_ref, q_ref, *refs, pages_per_step, pages_per_blk, nblk):
    k_refs = refs[:pages_per_step]
    o_ref, g_ref = refs[pages_per_step], refs[pages_per_step + 1]
    s = pl.program_id(1)
    blocks_per_step = pages_per_step // pages_per_blk
    bi = lax.broadcasted_iota(jnp.int32, (nblk, 128), 0)
    lane = lax.broadcasted_iota(jnp.int32, (blocks_per_step, 128), 1)

    slab = jnp.zeros((blocks_per_step, 128), F32)
    for h in range(N_HEADS):
        qc = q_ref[0, h * HEAD_DIM:(h + 1) * HEAD_DIM, :]
        rows = []
        for jb in range(blocks_per_step):
            prod = k_refs[jb * pages_per_blk][0, 0, h] * qc
            for i in range(1, pages_per_blk):
                prod = prod + k_refs[jb * pages_per_blk + i][0, 0, h] * qc
            rows.append(jnp.sum(prod, axis=0, keepdims=True))
        val = jnp.sum(jnp.concatenate(rows, axis=0), axis=1, keepdims=True) * (1.0 / MOBA_BLOCK)
        slab = jnp.where(lane == h, val, slab)
    g_ref[pl.ds(pl.multiple_of(s * blocks_per_step, blocks_per_step), blocks_per_step), :] = slab

    @pl.when(s == pl.num_programs(1) - 1)
    def _():
        gg = g_ref[...]
        rows = []
        for r in range(MOBA_TOPK):
            m = jnp.max(gg, axis=0, keepdims=True)
            idx = jnp.min(jnp.where(gg == m, bi, nblk), axis=0, keepdims=True)
            rows.append(idx)
            gg = jnp.where(bi == idx, -jnp.inf, gg)
        rows.append(jnp.zeros((8 - MOBA_TOPK, 128), jnp.int32))
        o_ref[0] = jnp.concatenate(rows, axis=0)


def _select_blocks(page_table, q, cache_kt, layer):
    bsz, npg = page_table.shape
    page_size = cache_kt.shape[4]
    pages_per_blk = MOBA_BLOCK // page_size
    nblk = npg // pages_per_blk
    pages_per_step = pages_per_blk * _row_tile(nblk, 8)
    n_steps = npg // pages_per_step

    def page_spec(i):
        return pl.BlockSpec((1, 1, N_HEADS, HEAD_DIM, page_size),
                            lambda b, s, pt: (layer, pt[b, s * pages_per_step + i], 0, 0, 0))

    out = pl.pallas_call(
        functools.partial(_gate_select_kernel, pages_per_step=pages_per_step, pages_per_blk=pages_per_blk, nblk=nblk),
        grid_spec=pltpu.PrefetchScalarGridSpec(
            num_scalar_prefetch=1,
            grid=(bsz, n_steps),
            in_specs=[pl.BlockSpec((1, D_GRP, 1), lambda b, s, pt: (b, 0, 0))]
                     + [page_spec(i) for i in range(pages_per_step)],
            out_specs=pl.BlockSpec((1, 8, 128), lambda b, s, pt: (b, 0, 0)),
            scratch_shapes=[pltpu.VMEM((nblk, 128), F32)]),
        out_shape=jax.ShapeDtypeStruct((bsz, 8, 128), jnp.int32),
        compiler_params=_params(("arbitrary", "arbitrary")),
        name="select_blocks",
    )(page_table, q.reshape(bsz, D_GRP, 1), *([cache_kt] * pages_per_step))
    return out[:, :MOBA_TOPK, :N_HEADS]


def _decode_kernel(top_ref, pt_ref, q_ref, kn_ref, vn_ref, *refs, page_size, past_len):
    pages_per_blk = MOBA_BLOCK // page_size
    npg = MOBA_TOPK * pages_per_blk
    k_refs = refs[:N_HEADS * npg]
    v_refs = refs[N_HEADS * npg:2 * N_HEADS * npg]
    o_ref = refs[2 * N_HEADS * npg]
    b = pl.program_id(0)
    lane = lax.broadcasted_iota(jnp.int32, (1, page_size), 1)
    for h in range(N_HEADS):
        rs = slice(h * HEAD_DIM, (h + 1) * HEAD_DIM)
        qc = q_ref[0, rs, :] * (HEAD_DIM ** -0.5)
        s_own = jnp.sum(qc * kn_ref[0, rs, :], axis=0, keepdims=True)
        ss = []
        for i in range(npg):
            blk = top_ref[b, (i // pages_per_blk) * N_HEADS + h]
            dist = (past_len - blk * MOBA_BLOCK - (i % pages_per_blk) * page_size - lane).astype(F32)
            kt = k_refs[h * npg + i][0, 0, 0]
            ss.append(jnp.sum(kt * qc, axis=0, keepdims=True) - ALIBI_SLOPES[h] * dist)
        m = s_own
        for s in ss:
            m = jnp.maximum(m, jnp.max(s, axis=-1, keepdims=True))
        p_own = jnp.exp(s_own - m)
        l = p_own
        acc = jnp.zeros((HEAD_DIM, page_size), F32)
        for i in range(npg):
            p = jnp.exp(ss[i] - m)
            l = l + jnp.sum(p, axis=-1, keepdims=True)
            acc = acc + v_refs[h * npg + i][0, 0, 0] * p
        o_ref[0, rs, :] = (jnp.sum(acc, axis=-1, keepdims=True) + p_own * vn_ref[0, rs, :]) / l


def _attn_decode(top, page_table, q, k_new, v_new, cache_kt, cache_vt, layer):
    bsz, npg_seq = page_table.shape
    page_size = cache_kt.shape[4]
    pages_per_blk = MOBA_BLOCK // page_size
    npg = MOBA_TOPK * pages_per_blk
    past_len = npg_seq * page_size

    def page_spec(h, i):
        def imap(b, top_ref, pt_ref):
            blk = top_ref[b, (i // pages_per_blk) * N_HEADS + h]
            return (layer, pt_ref[b, blk * pages_per_blk + i % pages_per_blk], h, 0, 0)
        return pl.BlockSpec((1, 1, 1, HEAD_DIM, page_size), imap)

    col_spec = pl.BlockSpec((1, D_GRP, 1), lambda b, t, p: (b, 0, 0))
    col = lambda a: a.reshape(bsz, D_GRP, 1)
    page_specs = [page_spec(h, i) for h in range(N_HEADS) for i in range(npg)]
    out = pl.pallas_call(
        functools.partial(_decode_kernel, page_size=page_size, past_len=past_len),
        grid_spec=pltpu.PrefetchScalarGridSpec(
            num_scalar_prefetch=2,
            grid=(bsz,),
            in_specs=[col_spec, col_spec, col_spec] + page_specs * 2,
            out_specs=col_spec),
        out_shape=jax.ShapeDtypeStruct((bsz, D_GRP, 1), F32),
        compiler_params=_params(("arbitrary",)),
        name="attn_decode",
    )(top.reshape(bsz, MOBA_TOPK * N_HEADS), page_table, col(q), col(k_new), col(v_new),
      *([cache_kt] * (N_HEADS * npg)), *([cache_vt] * (N_HEADS * npg)))
    return out.reshape(bsz, D_GRP)


def _rwkv_kernel(f_ref, sh0_ref, s0_ref, mu_ref, w0_ref, w2_ref, a0_ref, a2_ref, g2_ref, kk_ref, ka_ref,
                 rk_ref, lg_ref, lb_ref, o_ref, s_out_ref, s_ref, prev_ref, *, chunk, valid_len, bf16):
    i = pl.program_id(1)
    nb, tt, fd = f_ref.shape
    n_chunks = tt // chunk
    ht = N_HEADS * chunk
    rows = nb * tt
    n_pc = 2 if bf16 else 3

    @pl.when(i == 0)
    def _():
        s_ref[...] = s0_ref[...]
        prev_ref[...] = sh0_ref[...]

    f = f_ref[...].reshape(rows, fd)
    row = lax.broadcasted_iota(jnp.int32, (rows, 1), 0)
    prev = pltpu.roll(f, 1, 0)
    for b in range(nb):
        prev = jnp.where(row == b * tt, prev_ref[b], prev)
        prev_ref[b] = f[(b + 1) * tt - 1:(b + 1) * tt, :]
    m = f + (prev - f) * mu_ref[...]
    r = m[:, 0:D_GRP]
    k = m[:, D_GRP:2 * D_GRP]
    v = m[:, 2 * D_GRP:3 * D_GRP]
    c0 = 3 * D_GRP
    dr = w2_ref.shape[0]
    ar = a2_ref.shape[0]
    w_lo = m[:, c0:c0 + dr]
    a_lo = m[:, c0 + dr:c0 + dr + ar]
    g_lo = m[:, c0 + dr + ar:]
    y = -(w0_ref[...] + (_mm_split(jnp.tanh(w_lo), w2_ref[...]) if bf16 else _mm(jnp.tanh(w_lo), w2_ref[...], False)))
    softplus = jnp.maximum(y, 0.0) + jnp.log1p(jnp.exp(-jnp.abs(y)))
    logw = -jnp.exp(-softplus - 0.5)
    a = _sigmoid(a0_ref[...] + (_mm_split(a_lo, a2_ref[...]) if bf16 else _mm(a_lo, a2_ref[...], False)))
    g = _mm(_sigmoid(g_lo), g2_ref[...], bf16)

    lane_h = lax.broadcasted_iota(jnp.int32, (1, D_GRP), 1) // HEAD_DIM
    same_head = ((lax.broadcasted_iota(jnp.int32, (D_GRP, D_GRP), 0) // HEAD_DIM)
                 == (lax.broadcasted_iota(jnp.int32, (D_GRP, D_GRP), 1) // HEAD_DIM))
    e_head = same_head.astype(F32)
    kk = k * kk_ref[...]
    kkn = kk * lax.rsqrt(jnp.maximum(_mm_pieces(kk * kk, e_head, n_pc), 1e-12))
    k2 = k * (1.0 + (a - 1.0) * ka_ref[...])
    bb = kkn * a
    if valid_len is not None:
        ok = (i * tt + row % tt) < valid_len
        logw = jnp.where(ok, logw, 0.0)
        kkn = jnp.where(ok, kkn, 0.0)
        k2m = jnp.where(ok, k2, 0.0)
        bb = jnp.where(ok, bb, 0.0)
        vm = jnp.where(ok, v, 0.0)
    else:
        k2m, vm = k2, v

    ri = lax.broadcasted_iota(jnp.int32, (ht, ht), 0)
    ci = lax.broadcasted_iota(jnp.int32, (ht, ht), 1)
    same_blk = (ri // chunk) == (ci // chunk)
    strict = same_blk & ((ri % chunk) > (ci % chunk))
    incl = same_blk & ((ri % chunk) >= (ci % chunk))
    eye = (ri == ci).astype(F32)
    tril = (lax.broadcasted_iota(jnp.int32, (chunk, chunk), 0)
            >= lax.broadcasted_iota(jnp.int32, (chunk, chunk), 1)).astype(F32)

    def stack(x):
        return jnp.concatenate([jnp.where(lane_h == h, x, 0.0) for h in range(N_HEADS)], axis=0)

    def tile(x):
        return jnp.concatenate([x] * N_HEADS, axis=0)

    def unstack(x):
        out = jnp.zeros((chunk, D_GRP), F32)
        for h in range(N_HEADS):
            out = jnp.where(lane_h == h, x[h * chunk:(h + 1) * chunk], out)
        return out

    cat = lambda parts: parts[0] if len(parts) == 1 else jnp.concatenate(parts, axis=0)
    chunks = [[slice(b * tt + c * chunk, b * tt + (c + 1) * chunk) for b in range(nb)] for c in range(n_chunks)]
    flat = [rs for b in range(nb) for rs in (chunks[c][b] for c in range(n_chunks))]

    cum = cat([_mm_pieces_l(tril, logw[rs], 3) for rs in flat])
    cum_end = cat([jnp.broadcast_to(cum[rs.stop - 1:rs.stop], (chunk, D_GRP)) for rs in flat])
    inv_p = jnp.exp(-cum)
    to_end = jnp.exp(cum_end - cum)
    kap = kkn * jnp.exp(cum - logw)
    khat = k2m * inv_p
    bhat = bb * inv_p
    rhat = r * jnp.exp(cum)
    k_end = k2m * to_end
    b_end = bb * to_end
    every = [rs for step in chunks for rs in step]
    stk = {rs.start: (stack(kap[rs]), stack(rhat[rs]), stack(khat[rs]), stack(bhat[rs])) for rs in every}
    lmat = {rs.start: jnp.where(strict, _mm_nt(stk[rs.start][0], stk[rs.start][3], bf16), 0.0) for rs in every}
    akk = {rs.start: jnp.where(strict, _mm_nt(stk[rs.start][0], stk[rs.start][2], bf16), 0.0) for rs in every}
    ark = {rs.start: jnp.where(incl, _mm_nt(stk[rs.start][1], stk[rs.start][2], bf16), 0.0) for rs in every}
    arb = {rs.start: jnp.where(incl, _mm_nt(stk[rs.start][1], stk[rs.start][3], bf16), 0.0) for rs in every}
    v_t = {rs.start: tile(vm[rs]) for rs in every}
    akk_v = {key: _mm(akk[key], v_t[key], bf16) for key in akk}
    ark_v = {key: _mm(ark[key], v_t[key], bf16) for key in ark}
    minv = {key: eye - lmat[key] for key in lmat}
    lp = dict(lmat)
    for _ in range(int(math.log2(chunk)) - 1):
        lp = {key: _mm(lp[key], lp[key], bf16) for key in lp}
        minv = {key: _mm(minv[key], eye + lp[key], bf16) for key in minv}

    state = [s_ref[b] for b in range(nb)]
    o_parts = {}
    for step in chunks:
        keys = [rs.start for rs in step]
        kap_s = [_mm_nt(kap[rs], state[b], bf16) for b, rs in enumerate(step)]
        r_s = [_mm_nt(rhat[rs], state[b], bf16) for b, rs in enumerate(step)]
        u_t = [_mm(minv[key], tile(kap_s[b]) + akk_v[key], bf16) for b, key in enumerate(keys)]
        o_t = [tile(r_s[b]) + ark_v[key] - _mm(arb[key], u_t[b], bf16) for b, key in enumerate(keys)]
        upd = [_mm(vm[rs].T, k_end[rs], bf16) - _mm(unstack(u_t[b]).T, b_end[rs], bf16) for b, rs in enumerate(step)]
        for b, rs in enumerate(step):
            state[b] = state[b] * jnp.exp(cum[rs.stop - 1:rs.stop]) + jnp.where(same_head, upd[b], 0.0)
            o_parts[rs.start] = unstack(o_t[b])
    for b in range(nb):
        s_ref[b] = state[b]

    o = cat([o_parts[rs.start] for rs in flat])
    mean = _mm_pieces(o, e_head, n_pc) * (1.0 / HEAD_DIM)
    d = o - mean
    var = _mm_pieces(d * d, e_head, n_pc) * (1.0 / HEAD_DIM)
    on = d * lax.rsqrt(var + RWKV_LN_EPS) * lg_ref[...] + lb_ref[...]
    bonus = _mm_pieces(r * k2 * rk_ref[...], e_head, n_pc) * v
    o_ref[...] = ((on + bonus) * g).reshape(nb, tt, D_GRP)

    @pl.when(i == pl.num_programs(1) - 1)
    def _():
        s_out_ref[...] = s_ref[...]


def _rwkv(f, shift0, s0_bd, lp, valid_len, bf16):
    b, lpad, fd = f.shape
    tt = _row_tile(lpad, 256)
    chunk = min(64, tt)
    nb = _row_tile(b, max(1, 512 // tt))
    vec = lambda a: a.reshape(1, -1)
    pspec = lambda a: pl.BlockSpec(a.shape, lambda bi, ti: (0,) * a.ndim)
    plist = [vec(lp['mu_shift']), vec(lp['w0']), lp['w2'], vec(lp['a0']), lp['a2'], lp['g2'], vec(lp['k_k']),
             vec(lp['k_a']), vec(lp['r_k']), vec(lp['lnx_g']), vec(lp['lnx_b'])]
    return pl.pallas_call(
        functools.partial(_rwkv_kernel, chunk=chunk, valid_len=None if valid_len == lpad else valid_len, bf16=bf16),
        grid=(b // nb, lpad // tt),
        in_specs=[pl.BlockSpec((nb, tt, fd), lambda bi, ti: (bi, ti, 0)),
                  pl.BlockSpec((nb, 1, fd), lambda bi, ti: (bi, 0, 0)),
                  pl.BlockSpec((nb, D_GRP, D_GRP), lambda bi, ti: (bi, 0, 0))] + [pspec(a) for a in plist],
        out_specs=[pl.BlockSpec((nb, tt, D_GRP), lambda bi, ti: (bi, ti, 0)),
                   pl.BlockSpec((nb, D_GRP, D_GRP), lambda bi, ti: (bi, 0, 0))],
        out_shape=[jax.ShapeDtypeStruct((b, lpad, D_GRP), F32), jax.ShapeDtypeStruct((b, D_GRP, D_GRP), F32)],
        scratch_shapes=[pltpu.VMEM((nb, D_GRP, D_GRP), F32), pltpu.VMEM((nb, 1, fd), F32)],
        compiler_params=_params(("parallel", "arbitrary")),
        name="rwkv",
    )(f, shift0.reshape(b, 1, fd), s0_bd, *plist)


def _s5_discretise(lre, lim, ldt):
    step = jnp.exp(ldt)
    mag = jnp.exp(lre * step)
    are, aim = mag * jnp.cos(lim * step), mag * jnp.sin(lim * step)
    den = lre * lre + lim * lim
    return are, aim, ((are - 1.0) * lre + aim * lim) / den, (aim * lre - (are - 1.0) * lim) / den


def _s5_gate(y, u, dsk, wglu, bglu, bf16):
    y = y + dsk * u
    y = 0.5 * y * (1.0 + jnp.tanh(math.sqrt(2.0 / math.pi) * (y + 0.044715 * (y * y * y))))
    return y * _sigmoid(_mm(y, wglu, bf16) + bglu)


def _s5_step_kernel(u_ref, h0r_ref, h0i_ref, lre_ref, lim_ref, ldt_ref, bblk_ref, cre_ref, cim_ref, dsk_ref,
                    wglu_ref, bglu_ref, y_ref, hr_out_ref, hi_out_ref, *, bf16):
    ns = h0r_ref.shape[1]
    are, aim, cr, ci = _s5_discretise(lre_ref[...], lim_ref[...], ldt_ref[...])
    u = u_ref[...]
    ub = _mm(u, bblk_ref[...], bf16)
    ubr, ubi = ub[:, :ns], ub[:, ns:]
    h0r, h0i = h0r_ref[...], h0i_ref[...]
    hre = are * h0r - aim * h0i + (cr * ubr - ci * ubi)
    him = are * h0i + aim * h0r + (cr * ubi + ci * ubr)
    hr_out_ref[...] = hre
    hi_out_ref[...] = him
    y = _mm(hre, cre_ref[...], bf16) - _mm(him, cim_ref[...], bf16)
    y_ref[...] = _s5_gate(y, u, dsk_ref[...], wglu_ref[...], bglu_ref[...], bf16)


def _s5_kernel(u_ref, h0r_ref, h0i_ref, lre_ref, lim_ref, ldt_ref, bblk_ref, cre_ref, cim_ref, dsk_ref,
               wglu_ref, bglu_ref, y_ref, hr_out_ref, hi_out_ref,
               ar_ref, ai_ref, cr_ref, ci_ref, pwr_ref, pwi_ref, hr_ref, hi_ref, sr_ref, si_ref, perm_ref,
               *, last_row, bf16):
    i = pl.program_id(1)
    tt = u_ref.shape[1]
    seg = tt // 8
    ns = ar_ref.shape[1]

    @pl.when(i == 0)
    def _():
        are, aim, cre0, cim0 = _s5_discretise(lre_ref[...], lim_ref[...], ldt_ref[...])
        cr_ref[...] = cre0
        ci_ref[...] = cim0
        ar_ref[...] = are
        ai_ref[...] = aim
        pwr_ref[0:8, :] = jnp.broadcast_to(are, (8, ns))
        pwi_ref[0:8, :] = jnp.broadcast_to(aim, (8, ns))
        n = 1
        while n < seg:
            tr, ti = pwr_ref[8 * n - 1:8 * n, :], pwi_ref[8 * n - 1:8 * n, :]
            xr, xi = pwr_ref[0:8 * n, :], pwi_ref[0:8 * n, :]
            pwr_ref[8 * n:16 * n, :] = xr * tr - xi * ti
            pwi_ref[8 * n:16 * n, :] = xr * ti + xi * tr
            n *= 2
        hr_ref[...] = h0r_ref[0]
        hi_ref[...] = h0i_ref[0]
        if seg > 1:
            rr = lax.broadcasted_iota(jnp.int32, (tt, tt), 0)
            cc = lax.broadcasted_iota(jnp.int32, (tt, tt), 1)
            perm_ref[...] = jnp.where((rr % 8) * seg + rr // 8 == cc, 1.0, 0.0).astype(BF16)

    def permute(x, inverse):
        if seg == 1:
            return x
        hi = x.astype(BF16)
        lo = (x - hi.astype(F32)).astype(BF16)
        pm = perm_ref[...]
        dn = (((0,), (0,)), ((), ())) if inverse else (((1,), (0,)), ((), ()))
        return (lax.dot_general(pm, hi, dn, preferred_element_type=F32)
                + lax.dot_general(pm, lo, dn, preferred_element_type=F32))

    u = permute(u_ref[0], False)
    ub = _mm(u, bblk_ref[...], bf16)
    ubr, ubi = ub[:, :ns], ub[:, ns:]
    cr, ci = cr_ref[...], ci_ref[...]
    sr_ref[...] = cr * ubr - ci * ubi
    si_ref[...] = cr * ubi + ci * ubr

    are, aim = ar_ref[...], ai_ref[...]

    def local(j, carry):
        hr, hi = carry
        r0 = pl.multiple_of(j * 8, 8)
        nr = are * hr - aim * hi + sr_ref[pl.ds(r0, 8), :]
        ni = are * hi + aim * hr + si_ref[pl.ds(r0, 8), :]
        sr_ref[pl.ds(r0, 8), :] = nr
        si_ref[pl.ds(r0, 8), :] = ni
        return nr, ni

    z8 = jnp.zeros((8, ns), F32)
    er, ei = lax.fori_loop(0, seg, local, (z8, z8))

    pr_end, pi_end = pwr_ref[tt - 1:tt, :], pwi_ref[tt - 1:tt, :]
    cr_h, ci_h = hr_ref[...], hi_ref[...]
    ent_r, ent_i = [], []
    for s in range(8):
        ent_r.append(cr_h)
        ent_i.append(ci_h)
        cr_h, ci_h = (er[s:s + 1] + pr_end * cr_h - pi_end * ci_h,
                      ei[s:s + 1] + pr_end * ci_h + pi_end * cr_h)
    hr_ref[...] = cr_h
    hi_ref[...] = ci_h
    rep = lambda rows: jnp.broadcast_to(jnp.concatenate(rows, axis=0)[None], (seg, 8, ns)).reshape(tt, ns)
    ent_r, ent_i = rep(ent_r), rep(ent_i)
    pr, pi = pwr_ref[...], pwi_ref[...]
    hre = sr_ref[...] + pr * ent_r - pi * ent_i
    him = si_ref[...] + pr * ent_i + pi * ent_r

    y = _mm(hre, cre_ref[...], bf16) - _mm(him, cim_ref[...], bf16)
    y_ref[0] = permute(_s5_gate(y, u, dsk_ref[...], wglu_ref[...], bglu_ref[...], bf16), True)

    @pl.when(i == pl.num_programs(1) - 1)
    def _():
        hr_out_ref[0] = hre[last_row:last_row + 1, :]
        hi_out_ref[0] = him[last_row:last_row + 1, :]


def _s5(u, h0r, h0i, lp, valid_len, bf16):
    b, lpad, d = u.shape
    g, n = lp['lam_re'].shape
    p = SSM_GROUP
    ns = g * n
    tt = _row_tile(lpad, 512)
    eye_g = jnp.eye(g, dtype=F32)
    bre = jnp.einsum('gnp,gh->gphn', lp['b_re'], eye_g).reshape(g * p, ns)
    bim = jnp.einsum('gnp,gh->gphn', lp['b_im'], eye_g).reshape(g * p, ns)
    bblk = jnp.concatenate([bre, bim], axis=1)
    cre = jnp.einsum('gpn,gh->gnhp', lp['c_re'], eye_g).reshape(ns, g * p)
    cim = jnp.einsum('gpn,gh->gnhp', lp['c_im'], eye_g).reshape(ns, g * p)
    ldt = jnp.repeat(lp['log_dt'], n).reshape(1, ns)
    plist = [lp['lam_re'].reshape(1, ns), lp['lam_im'].reshape(1, ns), ldt, bblk, cre, cim,
             lp['d_skip'].reshape(1, d), lp['w_glu'], lp['b_glu'].reshape(1, d)]
    if lpad == 1:
        whole = lambda a: pl.BlockSpec(a.shape, lambda: (0,) * a.ndim)
        args = [u.reshape(b, d), h0r.reshape(b, ns), h0i.reshape(b, ns)] + plist
        y, hr, hi = pl.pallas_call(
            functools.partial(_s5_step_kernel, bf16=bf16),
            in_specs=[whole(a) for a in args],
            out_specs=[whole(jax.ShapeDtypeStruct((b, w), F32)) for w in (d, ns, ns)],
            out_shape=[jax.ShapeDtypeStruct((b, w), F32) for w in (d, ns, ns)],
            compiler_params=pltpu.CompilerParams(vmem_limit_bytes=VMEM_LIMIT),
            name="s5_step",
        )(*args)
        return y.reshape(b, 1, d), hr.reshape(b, 1, ns), hi.reshape(b, 1, ns)
    pspec = lambda a: pl.BlockSpec(a.shape, lambda bi, ti: (0,) * a.ndim)
    st_spec = pl.BlockSpec((1, 1, ns), lambda bi, ti: (bi, 0, 0))
    seg = tt // 8
    t_last = (valid_len - 1) % tt
    vm = lambda r: pltpu.VMEM((r, ns), F32)
    return pl.pallas_call(
        functools.partial(_s5_kernel, last_row=8 * (t_last % seg) + t_last // seg, bf16=bf16),
        grid=(b, lpad // tt),
        in_specs=[pl.BlockSpec((1, tt, d), lambda bi, ti: (bi, ti, 0)), st_spec, st_spec]
                 + [pspec(a) for a in plist],
        out_specs=[pl.BlockSpec((1, tt, d), lambda bi, ti: (bi, ti, 0)), st_spec, st_spec],
        out_shape=[jax.ShapeDtypeStruct((b, lpad, d), F32), jax.ShapeDtypeStruct((b, 1, ns), F32),
                   jax.ShapeDtypeStruct((b, 1, ns), F32)],
        scratch_shapes=[vm(1), vm(1), vm(1), vm(1), vm(tt), vm(tt), vm(1), vm(1), vm(tt), vm(tt),
                        pltpu.VMEM((tt, tt), BF16)],
        compiler_params=_params(("arbitrary", "arbitrary")),
        name="s5",
    )(u, h0r, h0i, *plist)


def _pool_kernel(u_ref, hist_ref, wp_ref, sc_ref, o_ref, ext_ref, *, pos0):
    i = pl.program_id(1)
    tt = u_ref.shape[1]
    hpad = 16

    @pl.when(i == 0)
    def _():
        ext_ref[0:hpad, :] = hist_ref[0]

    u = u_ref[0]
    ext_ref[hpad:hpad + tt, :] = u
    grp = lax.broadcasted_iota(jnp.int32, (1, D_GRP), 1) // (D_GRP // len(POOL_WINDOWS))
    pos = pos0 + i * tt + lax.broadcasted_iota(jnp.int32, (tt, 1), 0)
    acc = u
    pooled = jnp.zeros_like(u)
    d = 1
    for gi, w in enumerate(POOL_WINDOWS):
        while d < w:
            acc = acc + ext_ref[hpad - d:hpad - d + tt, :]
            d += 1
        cnt = jnp.minimum(pos + 1, w).astype(F32)
        pooled = jnp.where(grp == gi, acc / cnt, pooled)
    pooled = pooled - u
    o_ref[0] = _mm(pooled, wp_ref[...], False) * sc_ref[...]
    ext_ref[0:hpad, :] = ext_ref[tt:tt + hpad, :]


def _pool(u, hist16, lp, pos0):
    b, lpad, d = u.shape
    tt = _row_tile(lpad, 512)
    nw = len(POOL_WINDOWS)
    gw = d // nw
    wp = jnp.einsum('gcd,gh->gchd', lp['w_pool'], jnp.eye(nw, dtype=F32)).reshape(d, d)
    return pl.pallas_call(
        functools.partial(_pool_kernel, pos0=pos0),
        grid=(b, lpad // tt),
        in_specs=[pl.BlockSpec((1, tt, d), lambda bi, ti: (bi, ti, 0)),
                  pl.BlockSpec((1, 16, d), lambda bi, ti: (bi, 0, 0)),
                  pl.BlockSpec((d, d), lambda bi, ti: (0, 0)),
                  pl.BlockSpec((1, d), lambda bi, ti: (0, 0))],
        out_specs=pl.BlockSpec((1, tt, d), lambda bi, ti: (bi, ti, 0)),
        out_shape=jax.ShapeDtypeStruct((b, lpad, d), F32),
        scratch_shapes=[pltpu.VMEM((tt + 16, d), F32)],
        compiler_params=_params(("parallel", "arbitrary")),
        name="pool",
    )(u, hist16, wp, lp['pool_scale'].reshape(1, d))


def _bd_from_heads(s):
    b = s.shape[0]
    return jnp.einsum('bhvk,hg->bhvgk', s, jnp.eye(N_HEADS, dtype=s.dtype)).reshape(b, D_GRP, D_GRP)


def _heads_from_bd(s):
    b = s.shape[0]
    s5 = s.reshape(b, N_HEADS, HEAD_DIM, N_HEADS, HEAD_DIM)
    return jnp.stack([s5[:, h, :, h, :] for h in range(N_HEADS)], axis=1)


def _pad_rows(a, lpad):
    return a if a.shape[1] == lpad else jnp.pad(a, ((0, 0), (0, lpad - a.shape[1]), (0, 0)))


def _run_trunk(x, pos0, cache_k, cache_v, page_table, wkv0, shift0, ssm_re0, ssm_im0, pool0, P, final_g,
               bf16_small):
    bsz, l, d = x.shape
    depth = P['w_in'].shape[0]
    n = bsz * l
    lpad = -(-l // 8) * 8
    fd = shift0.shape[-1]
    widths = (D_GRP, D_GRP, D_GRP, fd, D_GRP, D_GRP)
    decode = cache_k is not None
    wdt = F32 if decode else BF16
    big = {name: P[name].astype(wdt) for name in ('w_in', 'w_out', 'w_gate', 'w_up', 'w_down')}
    if decode:
        page_size = cache_k.shape[2]
        assert (page_table.shape[1] * page_size) % MOBA_BLOCK == 0 and page_table.shape[1] * page_size >= MOBA_TOPK * MOBA_BLOCK
        ck = jnp.transpose(cache_k, (0, 1, 3, 4, 2))
        cv = jnp.transpose(cache_v, (0, 1, 3, 4, 2))
    xf = x.reshape(n, d)
    outs = []
    for li in range(depth):
        lp = {name: arr[li] for name, arr in P.items()}
        q, k, v, f, us, up, *aux = _inproj(xf, P['norm1_g'], big['w_in'], li, widths, not decode)
        if decode:
            top = _select_blocks(page_table, q, ck, li)
            att = _attn_decode(top, page_table, q, k, v, ck, cv, li)
        else:
            kb, vb, km = aux
            att = _attn_prompt(q.reshape(bsz, l, D_GRP), kb.reshape(bsz, l, D_GRP), vb.reshape(bsz, l, D_GRP),
                               km.reshape(bsz, l // MOBA_BLOCK, D_GRP)).reshape(n, D_GRP)
        f3 = f.reshape(bsz, l, fd)
        rw, s_new = _rwkv(_pad_rows(f3, lpad), shift0[li], _bd_from_heads(wkv0[li]), lp, l, bf16_small)
        g_n = ssm_re0.shape[2] * ssm_re0.shape[3]
        us3 = us.reshape(bsz, l, D_GRP)
        ss, hr, hi = _s5(us3 if l == 1 else _pad_rows(us3, lpad), ssm_re0[li].reshape(bsz, 1, g_n),
                         ssm_im0[li].reshape(bsz, 1, g_n), lp, l, bf16_small)
        up3 = up.reshape(bsz, l, D_GRP)
        hist16 = jnp.pad(pool0[li], ((0, 0), (1, 0), (0, 0)))
        pm = _pool(_pad_rows(up3, lpad), hist16, lp, pos0)
        xf, = _post(att, rw[:, :l].reshape(n, D_GRP), ss[:, :l].reshape(n, D_GRP),
                    pm[:, :l].reshape(n, D_GRP), xf, P['mix_g'], big['w_out'], P['norm2_g'],
                    big['w_gate'], big['w_up'], big['w_down'], final_g, li, li == depth - 1)
        pool_new = jnp.concatenate([pool0[li], up3], axis=1)[:, -POOL_HIST:]
        outs.append((k.reshape(bsz, l, N_HEADS, HEAD_DIM), v.reshape(bsz, l, N_HEADS, HEAD_DIM),
                     _heads_from_bd(s_new), f3[:, -1], hr.reshape(ssm_re0.shape[1:]),
                     hi.reshape(ssm_im0.shape[1:]), pool_new))
    new_state = [jnp.stack(s, axis=0) for s in zip(*outs)]
    return xf.reshape(bsz, l, d), new_state


def kernel(x_prompt, x_sample, cache_k, cache_v, page_table, state_wkv, state_shift, state_ssm_re, state_ssm_im, state_pool, norm1_g, w_in, mu_shift, w0, w2, a0, a2, g2, k_k, k_a, r_k, lnx_g, lnx_b, lam_re, lam_im, log_dt, b_re, b_im, c_re, c_im, d_skip, w_glu, b_glu, w_pool, pool_scale, mix_g, w_out, norm2_g, w_gate, w_up, w_down, final_g):
    P = dict(norm1_g=norm1_g, w_in=w_in, mu_shift=mu_shift, w0=w0, w2=w2, a0=a0, a2=a2, g2=g2,
             k_k=k_k, k_a=k_a, r_k=r_k, lnx_g=lnx_g, lnx_b=lnx_b, lam_re=lam_re, lam_im=lam_im,
             log_dt=log_dt, b_re=b_re, b_im=b_im, c_re=c_re, c_im=c_im, d_skip=d_skip, w_glu=w_glu,
             b_glu=b_glu, w_pool=w_pool, pool_scale=pool_scale, mix_g=mix_g, w_out=w_out,
             norm2_g=norm2_g, w_gate=w_gate, w_up=w_up, w_down=w_down)
    dt = x_prompt.dtype
    bp = x_prompt.shape[0]
    depth = w_in.shape[0]
    z = lambda *s: jnp.zeros((depth, bp) + s, dt)
    y_p, (k_p, v_p, wkv_p, shift_p, sre_p, sim_p, pool_p) = _run_trunk(
        x_prompt, 0, None, None, None, z(*state_wkv.shape[2:]), z(state_shift.shape[2]),
        z(*state_ssm_re.shape[2:]), z(*state_ssm_im.shape[2:]), z(*state_pool.shape[2:]), P, final_g, True)
    past_len = page_table.shape[1] * cache_k.shape[2]
    y_s, (k_s, v_s, wkv_s, shift_s, sre_s, sim_s, pool_s) = _run_trunk(
        x_sample, past_len, cache_k, cache_v, page_table, state_wkv, state_shift, state_ssm_re,
        state_ssm_im, state_pool, P, final_g, False)
    return (y_p, y_s, k_p, v_p, k_s, v_s, wkv_p, wkv_s, shift_p, shift_s,
            sre_p, sim_p, sre_s, sim_s, pool_p, pool_s)
```

```python
import functools
import math

import jax
import jax.numpy as jnp
from jax import lax
from jax.experimental import pallas as pl
from jax.experimental.pallas import tpu as pltpu

F32 = jnp.float32
BF16 = jnp.bfloat16
HI = lax.Precision.HIGHEST

HEAD_DIM = 64
N_HEADS = 4
D_GRP = N_HEADS * HEAD_DIM
MOBA_BLOCK = 256
MOBA_TOPK = 3
RWKV_LN_EPS = 64e-5
RMS_EPS = 1e-6
NEG = -1e30
SSM_GROUP = 16
SSM_STATE = 64
POOL_WINDOWS = (2, 4, 8, 16)
POOL_HIST = 15
ALIBI_SLOPES = tuple(2.0 ** (-8.0 * (h + 1) / N_HEADS) for h in range(N_HEADS))
VMEM_LIMIT = 56 * 1024 * 1024


def _params(sem):
    return pltpu.CompilerParams(dimension_semantics=sem, vmem_limit_bytes=VMEM_LIMIT)


def _mm(a, b, bf16):
    if bf16:
        return jnp.dot(a.astype(BF16), b.astype(BF16), preferred_element_type=F32)
    return jnp.dot(a, b, precision=HI, preferred_element_type=F32)


def _mm_nt(a, b, bf16):
    dn = (((1,), (1,)), ((), ()))
    if bf16:
        return lax.dot_general(a.astype(BF16), b.astype(BF16), dn, preferred_element_type=F32)
    return lax.dot_general(a, b, dn, precision=HI, preferred_element_type=F32)


def _act(x, w_ref):
    return x.astype(BF16) if w_ref.dtype == BF16 else x


def _dot_w(a, w):
    if w.dtype == BF16:
        return jnp.dot(a, w, preferred_element_type=F32)
    return jnp.dot(a, w, precision=HI, preferred_element_type=F32)


def _mm_pieces(a, b_exact, n):
    bb = b_exact.astype(BF16)
    out, rest = None, a
    for _ in range(n):
        piece = rest.astype(BF16)
        rest = rest - piece.astype(F32)
        d = jnp.dot(piece, bb, preferred_element_type=F32)
        out = d if out is None else out + d
    return out


def _mm_pieces_l(a_exact, b, n):
    ab = a_exact.astype(BF16)
    out, rest = None, b
    for _ in range(n):
        piece = rest.astype(BF16)
        rest = rest - piece.astype(F32)
        d = jnp.dot(ab, piece, preferred_element_type=F32)
        out = d if out is None else out + d
    return out


def _mm_split(a, b):
    ah, bh = a.astype(BF16), b.astype(BF16)
    al, bl = (a - ah.astype(F32)).astype(BF16), (b - bh.astype(F32)).astype(BF16)
    dot = lambda x, y: jnp.dot(x, y, preferred_element_type=F32)
    return dot(ah, bh) + (dot(ah, bl) + dot(al, bh))


def _mm_nt_split(a, b):
    dn = (((1,), (1,)), ((), ()))
    ah, bh = a.astype(BF16), b.astype(BF16)
    al, bl = (a - ah.astype(F32)).astype(BF16), (b - bh.astype(F32)).astype(BF16)
    dot = lambda x, y: lax.dot_general(x, y, dn, preferred_element_type=F32)
    return dot(ah, bh) + (dot(ah, bl) + dot(al, bh))


def _rms(x, g):
    return x * lax.rsqrt(jnp.mean(x * x, axis=-1, keepdims=True) + RMS_EPS) * g


def _sigmoid(x):
    return 1.0 / (1.0 + jnp.exp(-x))


def _row_tile(n, target):
    t = min(n, target)
    while n % t:
        t //= 2
    return t


def _inproj_kernel(x_ref, g_ref, w_ref, *refs, widths, prompt):
    h = _act(_rms(x_ref[...], g_ref[0]), w_ref)
    c0 = 0
    zs = []
    for wd in widths:
        zs.append(_dot_w(h, w_ref[0, :, c0:c0 + wd]))
        c0 += wd
    if not prompt:
        for o_ref, z in zip(refs, zs):
            o_ref[...] = z
        return
    q_ref, f_ref, us_ref, up_ref, kb_ref, vb_ref, km_ref, kt_ref, vt_ref = refs[2:]
    q_ref[...] = zs[0]
    f_ref[...] = zs[3]
    us_ref[...] = zs[4]
    up_ref[...] = zs[5]
    kb_ref[...] = zs[1].astype(BF16)
    vb_ref[...] = zs[2].astype(BF16)
    for j in range(km_ref.shape[1]):
        km_ref[0, j:j + 1, :] = jnp.mean(zs[1][j * MOBA_BLOCK:(j + 1) * MOBA_BLOCK], axis=0, keepdims=True)
    kt_ref[0, 0] = zs[1].T
    vt_ref[0, 0] = zs[2].T


def _inproj(x, g, w, layer, widths, kv_t=None):
    n, d = x.shape
    tm = _row_tile(n, 512)
    row = lambda wd, dt=F32: (pl.BlockSpec((tm, wd), lambda i: (i, 0)), jax.ShapeDtypeStruct((n, wd), dt))
    in_specs = [pl.BlockSpec((tm, d), lambda i: (i, 0)),
                pl.BlockSpec((1, 1, d), lambda i: (layer, 0, 0)),
                pl.BlockSpec((1,) + w.shape[1:], lambda i: (layer, 0, 0), pipeline_mode=pl.Buffered(1))]
    args = [x, g.reshape(g.shape[0], 1, d), w]
    aliases = {}
    if kv_t is None:
        outs = [row(wd) for wd in widths]
    else:
        seq = kv_t[0].shape[3]
        assert tm % MOBA_BLOCK == 0 and seq % tm == 0
        bpt, tps = tm // MOBA_BLOCK, seq // tm
        kv_spec = pl.BlockSpec((1, 1, D_GRP, tm), lambda i: (layer, i // tps, 0, i % tps))
        outs = [row(widths[0]), row(widths[3]), row(widths[4]), row(widths[5]), row(D_GRP, BF16), row(D_GRP, BF16),
                (pl.BlockSpec((1, bpt, D_GRP), lambda i: (i, 0, 0)), jax.ShapeDtypeStruct((n // tm, bpt, D_GRP), F32)),
                (kv_spec, jax.ShapeDtypeStruct(kv_t[0].shape, F32)), (kv_spec, jax.ShapeDtypeStruct(kv_t[1].shape, F32))]
        in_specs += [pl.BlockSpec(memory_space=pl.ANY)] * 2
        args += list(kv_t)
        aliases = {3: len(outs) - 2, 4: len(outs) - 1}
    return pl.pallas_call(
        functools.partial(_inproj_kernel, widths=widths, prompt=kv_t is not None),
        grid=(n // tm,),
        in_specs=in_specs,
        out_specs=[o[0] for o in outs],
        out_shape=[o[1] for o in outs],
        input_output_aliases=aliases,
        compiler_params=_params(("parallel",)),
        name="inproj",
    )(*args)


def _post_kernel(att_ref, rw_ref, ss_ref, pm_ref, x_ref, mixg_ref, wout_ref, n2g_ref,
                 wg_ref, wu_ref, wd_ref, fing_ref, *out_refs, final):
    acc = x_ref[...]
    for i, ref in enumerate((att_ref, rw_ref, ss_ref, pm_ref)):
        c = slice(i * D_GRP, (i + 1) * D_GRP)
        acc = acc + _dot_w(_act(_rms(ref[...], mixg_ref[0, :, c]), wout_ref), wout_ref[0, c, :])
    h2 = _act(_rms(acc, n2g_ref[0]), wg_ref)
    gt = _dot_w(h2, wg_ref[0])
    up = _dot_w(h2, wu_ref[0])
    acc = acc + _dot_w(_act(gt * _sigmoid(gt) * up, wd_ref), wd_ref[0])
    out_refs[0][...] = _rms(acc, fing_ref[...]) if final else acc


def _post(att, rw, ss, pm, x, mix_g, wout, n2g, wg, wu, wd, fin_g, layer, final):
    n, d = x.shape
    tm = _row_tile(n, 512)
    row = lambda w: pl.BlockSpec((tm, w), lambda i: (i, 0))
    full = lambda a: pl.BlockSpec((1,) + a.shape[1:], lambda i: (layer, 0, 0), pipeline_mode=pl.Buffered(1))
    vec = lambda: pl.BlockSpec((1, 1, d), lambda i: (layer, 0, 0))
    v3 = lambda a: a.reshape(a.shape[0], 1, d)
    return pl.pallas_call(
        functools.partial(_post_kernel, final=final),
        grid=(n // tm,),
        in_specs=[row(D_GRP), row(D_GRP), row(D_GRP), row(D_GRP), row(d), vec(), full(wout), vec(),
                  full(wg), full(wu), full(wd), pl.BlockSpec((1, d), lambda i: (0, 0))],
        out_specs=[row(d)],
        out_shape=[jax.ShapeDtypeStruct((n, d), F32)],
        compiler_params=_params(("parallel",)),
        name="post",
    )(att, rw, ss, pm, x, v3(mix_g), wout, v3(n2g), wg, wu, wd, fin_g.reshape(1, d))


def _attn_kernel(q_ref, kb_ref, vb_ref, km_ref, o_ref, *, nblk):
    qi = pl.program_id(1)
    blk = MOBA_BLOCK
    hd2 = 2 * HEAD_DIM
    n_pairs = N_HEADS // 2
    f_row, f_col, f_inv = nblk, nblk + 1, nblk + 2
    lane = lax.broadcasted_iota(jnp.int32, (blk, hd2), 1)
    rowf = lax.broadcasted_iota(jnp.int32, (blk, hd2), 0).astype(F32)
    causal = (lax.broadcasted_iota(jnp.int32, (blk, blk), 0) >= lax.broadcasted_iota(jnp.int32, (blk, blk), 1))
    bi = lax.broadcasted_iota(jnp.int32, (nblk, blk), 0)
    bif = bi.astype(F32)
    km_lane = lax.broadcasted_iota(jnp.int32, (nblk, hd2), 1)
    k_static = jnp.where(lane == f_row, 1.0, jnp.where(lane == f_col, rowf, 0.0))

    def key_feat(j, fully_past):
        return jnp.where(lane == jnp.where(fully_past, j, f_inv), 1.0, k_static).astype(BF16)

    gates = []
    for hp in range(n_pairs):
        cs = slice(hp * hd2, (hp + 1) * hd2)
        km2 = km_ref[0, :, cs]
        km_heads = jnp.concatenate([jnp.where((km_lane // HEAD_DIM) == hh, km2, 0.0) for hh in range(2)], axis=0)
        gates.append(_mm_nt_split(km_heads, q_ref[0, :, cs]))

    qps = []
    for h in range(N_HEADS):
        cs = slice((h // 2) * hd2, (h // 2 + 1) * hd2)
        q2 = q_ref[0, :, cs]
        slope = ALIBI_SLOPES[h]
        gate = gates[h // 2][(h % 2) * nblk:(h % 2 + 1) * nblk]
        g = jnp.where(bi < qi, gate, NEG)
        sel = bi == qi
        for r in range(MOBA_TOPK):
            m = jnp.max(g, axis=0, keepdims=True)
            idx = jnp.min(jnp.where(g == m, bif, float(nblk)), axis=0, keepdims=True)
            hit = bif == idx
            sel = sel | (hit & (qi > r))
            g = jnp.where(hit, -jnp.inf, g)
        rt = jnp.where(sel, (-slope * blk) * (qi - bi).astype(F32), NEG)
        rt = jnp.concatenate([rt, jnp.zeros((hd2 - nblk, blk), F32)], axis=0).T
        feat = jnp.where(lane == f_row, -slope * rowf,
                         jnp.where(lane == f_col, slope, jnp.where(lane == f_inv, NEG, rt)))
        qm = jnp.where((lane // HEAD_DIM) == h % 2, q2, 0.0) * (HEAD_DIM ** -0.5)
        qps.append(jnp.concatenate([qm.astype(BF16), feat.astype(BF16)], axis=1))

    r_own = pl.multiple_of(qi * blk, blk)
    own_feat = key_feat(qi, True)
    kps = [jnp.concatenate([kb_ref[0, pl.ds(r_own, blk), slice(hp * hd2, (hp + 1) * hd2)], own_feat], axis=1)
           for hp in range(n_pairs)]
    ss = [_mm_nt(qps[h], kps[h // 2], True) for h in range(N_HEADS)]
    stats, ps = [], []
    for h in range(N_HEADS):
        s = jnp.where(causal, ss[h], NEG)
        m0 = jnp.max(s, axis=-1, keepdims=True)
        p = jnp.exp(s - m0)
        stats.append((m0, jnp.sum(p, axis=-1, keepdims=True)))
        ps.append(p.astype(BF16))
    init = []
    for h in range(N_HEADS):
        cs = slice((h // 2) * hd2, (h // 2 + 1) * hd2)
        init.append(stats[h] + (jnp.dot(ps[h], vb_ref[0, pl.ds(r_own, blk), cs], preferred_element_type=F32),))

    def body(t, carry):
        r0 = pl.multiple_of(t * (2 * blk), 2 * blk)
        kfeat = jnp.concatenate([key_feat(2 * t, True), key_feat(2 * t + 1, 2 * t + 1 < qi)], axis=0)
        kps = [jnp.concatenate([kb_ref[0, pl.ds(r0, 2 * blk), slice(hp * hd2, (hp + 1) * hd2)], kfeat], axis=1)
               for hp in range(n_pairs)]
        ss = [_mm_nt(qps[h], kps[h // 2], True) for h in range(N_HEADS)]
        stats, ps = [], []
        for h in range(N_HEADS):
            m_i, l_i, _ = carry[h]
            m_new = jnp.maximum(m_i, jnp.max(ss[h], axis=-1, keepdims=True))
            p = jnp.exp(ss[h] - m_new)
            alpha = jnp.exp(m_i - m_new)
            stats.append((m_new, alpha * l_i + jnp.sum(p, axis=-1, keepdims=True), alpha))
            ps.append(p.astype(BF16))
        new = []
        for h in range(N_HEADS):
            cs = slice((h // 2) * hd2, (h // 2 + 1) * hd2)
            pv = jnp.dot(ps[h], vb_ref[0, pl.ds(r0, 2 * blk), cs], preferred_element_type=F32)
            new.append((stats[h][0], stats[h][1], stats[h][2] * carry[h][2] + pv))
        return tuple(new)

    res = lax.fori_loop(0, (qi + 1) // 2, body, tuple(init))
    for hp in range(n_pairs):
        o0 = res[2 * hp][2] / res[2 * hp][1]
        o1 = res[2 * hp + 1][2] / res[2 * hp + 1][1]
        o_ref[0, :, hp * hd2:(hp + 1) * hd2] = jnp.where((lane // HEAD_DIM) == 0, o0, o1)


def _attn_prompt(q, kb, vb, kmean):
    b, l, d = q.shape
    nblk = l // MOBA_BLOCK
    assert nblk + 3 <= 2 * HEAD_DIM
    return pl.pallas_call(
        functools.partial(_attn_kernel, nblk=nblk),
        grid=(b, nblk),
        in_specs=[pl.BlockSpec((1, MOBA_BLOCK, d), lambda i, j: (i, j, 0)),
                  pl.BlockSpec((1, l, d), lambda i, j: (i, 0, 0)),
                  pl.BlockSpec((1, l, d), lambda i, j: (i, 0, 0)),
                  pl.BlockSpec((1, nblk, d), lambda i, j: (i, 0, 0))],
        out_specs=pl.BlockSpec((1, MOBA_BLOCK, d), lambda i, j: (i, j, 0)),
        out_shape=jax.ShapeDtypeStruct((b, l, d), F32),
        compiler_params=_params(("parallel", "parallel")),
        name="attn_prompt",
    )(q, kb, vb, kmean)


def _gate_select_kernel(pt_ref, q_ref, *refs, pages_per_step, pages_per_blk, nblk):
    k_refs = refs[:pages_per_step]
    o_ref, g_ref = refs[pages_per_step], refs[pages_per_step + 1]
    s = pl.program_id(1)
    blocks_per_step = pages_per_step // pages_per_blk
    bi = lax.broadcasted_iota(jnp.int32, (nblk, 128), 0)
    lane = lax.broadcasted_iota(jnp.int32, (blocks_per_step, 128), 1)

    slab = jnp.zeros((blocks_per_step, 128), F32)
    for h in range(N_HEADS):
        qc = q_ref[0, h * HEAD_DIM:(h + 1) * HEAD_DIM, :]
        rows = []
        for jb in range(blocks_per_step):
            prod = k_refs[jb * pages_per_blk][0, 0, h] * qc
            for i in range(1, pages_per_blk):
                prod = prod + k_refs[jb * pages_per_blk + i][0, 0, h] * qc
            rows.append(jnp.sum(prod, axis=0, keepdims=True))
        val = jnp.sum(jnp.concatenate(rows, axis=0), axis=1, keepdims=True) * (1.0 / MOBA_BLOCK)
        slab = jnp.where(lane == h, val, slab)
    g_ref[pl.ds(pl.multiple_of(s * blocks_per_step, blocks_per_step), blocks_per_step), :] = slab

    @pl.when(s == pl.num_programs(1) - 1)
    def _():
        gg = g_ref[...]
        rows = []
        for r in range(MOBA_TOPK):
            m = jnp.max(gg, axis=0, keepdims=True)
            idx = jnp.min(jnp.where(gg == m, bi, nblk), axis=0, keepdims=True)
            rows.append(idx)
            gg = jnp.where(bi == idx, -jnp.inf, gg)
        rows.append(jnp.zeros((8 - MOBA_TOPK, 128), jnp.int32))
        o_ref[0] = jnp.concatenate(rows, axis=0)


def _select_blocks(page_table, q, cache_kt, layer):
    bsz, npg = page_table.shape
    page_size = cache_kt.shape[4]
    pages_per_blk = MOBA_BLOCK // page_size
    nblk = npg // pages_per_blk
    pages_per_step = pages_per_blk * _row_tile(nblk, 16)
    n_steps = npg // pages_per_step

    def page_spec(i):
        return pl.BlockSpec((1, 1, N_HEADS, HEAD_DIM, page_size),
                            lambda b, s, pt: (layer, pt[b, s * pages_per_step + i], 0, 0, 0))

    out = pl.pallas_call(
        functools.partial(_gate_select_kernel, pages_per_step=pages_per_step, pages_per_blk=pages_per_blk, nblk=nblk),
        grid_spec=pltpu.PrefetchScalarGridSpec(
            num_scalar_prefetch=1,
            grid=(bsz, n_steps),
            in_specs=[pl.BlockSpec((1, D_GRP, 1), lambda b, s, pt: (b, 0, 0))]
                     + [page_spec(i) for i in range(pages_per_step)],
            out_specs=pl.BlockSpec((1, 8, 128), lambda b, s, pt: (b, 0, 0)),
            scratch_shapes=[pltpu.VMEM((nblk, 128), F32)]),
        out_shape=jax.ShapeDtypeStruct((bsz, 8, 128), jnp.int32),
        compiler_params=_params(("arbitrary", "arbitrary")),
        name="select_blocks",
    )(page_table, q.reshape(bsz, D_GRP, 1), *([cache_kt] * pages_per_step))
    return out[:, :MOBA_TOPK, :N_HEADS]


def _decode_kernel(top_ref, pt_ref, q_ref, kn_ref, vn_ref, *refs, page_size, past_len):
    pages_per_blk = MOBA_BLOCK // page_size
    npg = MOBA_TOPK * pages_per_blk
    k_refs = refs[:N_HEADS * npg]
    v_refs = refs[N_HEADS * npg:2 * N_HEADS * npg]
    o_ref = refs[2 * N_HEADS * npg]
    b = pl.program_id(0)
    lane = lax.broadcasted_iota(jnp.int32, (1, page_size), 1)
    for h in range(N_HEADS):
        rs = slice(h * HEAD_DIM, (h + 1) * HEAD_DIM)
        qc = q_ref[0, rs, :] * (HEAD_DIM ** -0.5)
        s_own = jnp.sum(qc * kn_ref[0, rs, :], axis=0, keepdims=True)
        ss = []
        for i in range(npg):
            blk = top_ref[b, (i // pages_per_blk) * N_HEADS + h]
            dist = (past_len - blk * MOBA_BLOCK - (i % pages_per_blk) * page_size - lane).astype(F32)
            kt = k_refs[h * npg + i][0, 0, 0]
            ss.append(jnp.sum(kt * qc, axis=0, keepdims=True) - ALIBI_SLOPES[h] * dist)
        m = s_own
        for s in ss:
            m = jnp.maximum(m, jnp.max(s, axis=-1, keepdims=True))
        p_own = jnp.exp(s_own - m)
        l = p_own
        acc = jnp.zeros((HEAD_DIM, page_size), F32)
        for i in range(npg):
            p = jnp.exp(ss[i] - m)
            l = l + jnp.sum(p, axis=-1, keepdims=True)
            acc = acc + v_refs[h * npg + i][0, 0, 0] * p
        o_ref[0, rs, :] = (jnp.sum(acc, axis=-1, keepdims=True) + p_own * vn_ref[0, rs, :]) / l


def _attn_decode(top, page_table, q, k_new, v_new, cache_kt, cache_vt, layer):
    bsz, npg_seq = page_table.shape
    page_size = cache_kt.shape[4]
    pages_per_blk = MOBA_BLOCK // page_size
    npg = MOBA_TOPK * pages_per_blk
    past_len = npg_seq * page_size

    def page_spec(h, i):
        def imap(b, top_ref, pt_ref):
            blk = top_ref[b, (i // pages_per_blk) * N_HEADS + h]
            return (layer, pt_ref[b, blk * pages_per_blk + i % pages_per_blk], h, 0, 0)
        return pl.BlockSpec((1, 1, 1, HEAD_DIM, page_size), imap)

    col_spec = pl.BlockSpec((1, D_GRP, 1), lambda b, t, p: (b, 0, 0))
    col = lambda a: a.reshape(bsz, D_GRP, 1)
    page_specs = [page_spec(h, i) for h in range(N_HEADS) for i in range(npg)]
    out = pl.pallas_call(
        functools.partial(_decode_kernel, page_size=page_size, past_len=past_len),
        grid_spec=pltpu.PrefetchScalarGridSpec(
            num_scalar_prefetch=2,
            grid=(bsz,),
            in_specs=[col_spec, col_spec, col_spec] + page_specs * 2,
            out_specs=col_spec),
        out_shape=jax.ShapeDtypeStruct((bsz, D_GRP, 1), F32),
        compiler_params=_params(("arbitrary",)),
        name="attn_decode",
    )(top.reshape(bsz, MOBA_TOPK * N_HEADS), page_table, col(q), col(k_new), col(v_new),
      *([cache_kt] * (N_HEADS * npg)), *([cache_vt] * (N_HEADS * npg)))
    return out.reshape(bsz, D_GRP)


def _rwkv_kernel(f_ref, sh0_ref, s0_ref, mu_ref, w0_ref, w2_ref, a0_ref, a2_ref, g2_ref, kk_ref, ka_ref,
                 rk_ref, lg_ref, lb_ref, o_ref, s_out_ref, s_ref, prev_ref, *, chunk, valid_len, bf16):
    i = pl.program_id(1)
    nb, tt, fd = f_ref.shape
    n_chunks = tt // chunk
    ht = N_HEADS * chunk
    rows = nb * tt
    n_pc = 2 if bf16 else 3

    @pl.when(i == 0)
    def _():
        s_ref[...] = s0_ref[...]
        prev_ref[...] = sh0_ref[...]

    f = f_ref[...].reshape(rows, fd)
    row = lax.broadcasted_iota(jnp.int32, (rows, 1), 0)
    prev = pltpu.roll(f, 1, 0)
    for b in range(nb):
        prev = jnp.where(row == b * tt, prev_ref[b], prev)
        prev_ref[b] = f[(b + 1) * tt - 1:(b + 1) * tt, :]
    m = f + (prev - f) * mu_ref[...]
    r = m[:, 0:D_GRP]
    k = m[:, D_GRP:2 * D_GRP]
    v = m[:, 2 * D_GRP:3 * D_GRP]
    c0 = 3 * D_GRP
    dr = w2_ref.shape[0]
    ar = a2_ref.shape[0]
    w_lo = m[:, c0:c0 + dr]
    a_lo = m[:, c0 + dr:c0 + dr + ar]
    g_lo = m[:, c0 + dr + ar:]
    y = -(w0_ref[...] + (_mm_split(jnp.tanh(w_lo), w2_ref[...]) if bf16 else _mm(jnp.tanh(w_lo), w2_ref[...], False)))
    softplus = jnp.maximum(y, 0.0) + jnp.log1p(jnp.exp(-jnp.abs(y)))
    logw = -jnp.exp(-softplus - 0.5)
    a = _sigmoid(a0_ref[...] + (_mm_split(a_lo, a2_ref[...]) if bf16 else _mm(a_lo, a2_ref[...], False)))
    g = _mm(_sigmoid(g_lo), g2_ref[...], bf16)

    lane_h = lax.broadcasted_iota(jnp.int32, (1, D_GRP), 1) // HEAD_DIM
    same_head = ((lax.broadcasted_iota(jnp.int32, (D_GRP, D_GRP), 0) // HEAD_DIM)
                 == (lax.broadcasted_iota(jnp.int32, (D_GRP, D_GRP), 1) // HEAD_DIM))
    e_head = same_head.astype(F32)
    kk = k * kk_ref[...]
    kkn = kk * lax.rsqrt(jnp.maximum(_mm_pieces(kk * kk, e_head, n_pc), 1e-12))
    k2 = k * (1.0 + (a - 1.0) * ka_ref[...])
    bb = kkn * a
    if valid_len is not None:
        ok = (i * tt + row % tt) < valid_len
        logw = jnp.where(ok, logw, 0.0)
        kkn = jnp.where(ok, kkn, 0.0)
        k2m = jnp.where(ok, k2, 0.0)
        bb = jnp.where(ok, bb, 0.0)
        vm = jnp.where(ok, v, 0.0)
    else:
        k2m, vm = k2, v

    ri = lax.broadcasted_iota(jnp.int32, (ht, ht), 0)
    ci = lax.broadcasted_iota(jnp.int32, (ht, ht), 1)
    same_blk = (ri // chunk) == (ci // chunk)
    strict = same_blk & ((ri % chunk) > (ci % chunk))
    incl = same_blk & ((ri % chunk) >= (ci % chunk))
    eye = (ri == ci).astype(F32)
    tril = (lax.broadcasted_iota(jnp.int32, (chunk, chunk), 0)
            >= lax.broadcasted_iota(jnp.int32, (chunk, chunk), 1)).astype(F32)

    def stack(x):
        return jnp.concatenate([jnp.where(lane_h == h, x, 0.0) for h in range(N_HEADS)], axis=0)

    def tile(x):
        return jnp.concatenate([x] * N_HEADS, axis=0)

    def unstack(x):
        out = jnp.zeros((chunk, D_GRP), F32)
        for h in range(N_HEADS):
            out = jnp.where(lane_h == h, x[h * chunk:(h + 1) * chunk], out)
        return out

    cat = lambda parts: parts[0] if len(parts) == 1 else jnp.concatenate(parts, axis=0)
    chunks = [[slice(b * tt + c * chunk, b * tt + (c + 1) * chunk) for b in range(nb)] for c in range(n_chunks)]
    flat = [rs for b in range(nb) for rs in (chunks[c][b] for c in range(n_chunks))]

    cum = cat([_mm_pieces_l(tril, logw[rs], 3) for rs in flat])
    cum_end = cat([jnp.broadcast_to(cum[rs.stop - 1:rs.stop], (chunk, D_GRP)) for rs in flat])
    inv_p = jnp.exp(-cum)
    to_end = jnp.exp(cum_end - cum)
    kap = kkn * jnp.exp(cum - logw)
    khat = k2m * inv_p
    bhat = bb * inv_p
    rhat = r * jnp.exp(cum)
    k_end = k2m * to_end
    b_end = bb * to_end
    every = [rs for step in chunks for rs in step]
    stk = {rs.start: (stack(kap[rs]), stack(rhat[rs]), stack(khat[rs]), stack(bhat[rs])) for rs in every}
    lmat = {rs.start: jnp.where(strict, _mm_nt(stk[rs.start][0], stk[rs.start][3], bf16), 0.0) for rs in every}
    akk = {rs.start: jnp.where(strict, _mm_nt(stk[rs.start][0], stk[rs.start][2], bf16), 0.0) for rs in every}
    ark = {rs.start: jnp.where(incl, _mm_nt(stk[rs.start][1], stk[rs.start][2], bf16), 0.0) for rs in every}
    arb = {rs.start: jnp.where(incl, _mm_nt(stk[rs.start][1], stk[rs.start][3], bf16), 0.0) for rs in every}
    v_t = {rs.start: tile(vm[rs]) for rs in every}
    akk_v = {key: _mm(akk[key], v_t[key], bf16) for key in akk}
    ark_v = {key: _mm(ark[key], v_t[key], bf16) for key in ark}
    minv = {key: eye - lmat[key] for key in lmat}
    lp = dict(lmat)
    for _ in range(int(math.log2(chunk)) - 1):
        lp = {key: _mm(lp[key], lp[key], bf16) for key in lp}
        minv = {key: _mm(minv[key], eye + lp[key], bf16) for key in minv}

    state = [s_ref[b] for b in range(nb)]
    o_parts = {}
    for step in chunks:
        keys = [rs.start for rs in step]
        kap_s = [_mm_nt(kap[rs], state[b], bf16) for b, rs in enumerate(step)]
        r_s = [_mm_nt(rhat[rs], state[b], bf16) for b, rs in enumerate(step)]
        u_t = [_mm(minv[key], tile(kap_s[b]) + akk_v[key], bf16) for b, key in enumerate(keys)]
        o_t = [tile(r_s[b]) + ark_v[key] - _mm(arb[key], u_t[b], bf16) for b, key in enumerate(keys)]
        upd = [_mm(vm[rs].T, k_end[rs], bf16) - _mm(unstack(u_t[b]).T, b_end[rs], bf16) for b, rs in enumerate(step)]
        for b, rs in enumerate(step):
            state[b] = state[b] * jnp.exp(cum[rs.stop - 1:rs.stop]) + jnp.where(same_head, upd[b], 0.0)
            o_parts[rs.start] = unstack(o_t[b])
    for b in range(nb):
        s_ref[b] = state[b]

    o = cat([o_parts[rs.start] for rs in flat])
    mean = _mm_pieces(o, e_head, n_pc) * (1.0 / HEAD_DIM)
    d = o - mean
    var = _mm_pieces(d * d, e_head, n_pc) * (1.0 / HEAD_DIM)
    on = d * lax.rsqrt(var + RWKV_LN_EPS) * lg_ref[...] + lb_ref[...]
    bonus = _mm_pieces(r * k2 * rk_ref[...], e_head, n_pc) * v
    o_ref[...] = ((on + bonus) * g).reshape(nb, tt, D_GRP)

    @pl.when(i == pl.num_programs(1) - 1)
    def _():
        s_out_ref[...] = s_ref[...]


def _rwkv(f, shift0, s0_bd, lp, valid_len, bf16):
    b, lpad, fd = f.shape
    tt = _row_tile(lpad, 256)
    chunk = min(64, tt)
    nb = _row_tile(b, max(1, 512 // tt))
    vec = lambda a: a.reshape(1, -1)
    pspec = lambda a: pl.BlockSpec(a.shape, lambda bi, ti: (0,) * a.ndim)
    plist = [vec(lp['mu_shift']), vec(lp['w0']), lp['w2'], vec(lp['a0']), lp['a2'], lp['g2'], vec(lp['k_k']),
             vec(lp['k_a']), vec(lp['r_k']), vec(lp['lnx_g']), vec(lp['lnx_b'])]
    return pl.pallas_call(
        functools.partial(_rwkv_kernel, chunk=chunk, valid_len=None if valid_len == lpad else valid_len, bf16=bf16),
        grid=(b // nb, lpad // tt),
        in_specs=[pl.BlockSpec((nb, tt, fd), lambda bi, ti: (bi, ti, 0)),
                  pl.BlockSpec((nb, 1, fd), lambda bi, ti: (bi, 0, 0)),
                  pl.BlockSpec((nb, D_GRP, D_GRP), lambda bi, ti: (bi, 0, 0))] + [pspec(a) for a in plist],
        out_specs=[pl.BlockSpec((nb, tt, D_GRP), lambda bi, ti: (bi, ti, 0)),
                   pl.BlockSpec((nb, D_GRP, D_GRP), lambda bi, ti: (bi, 0, 0))],
        out_shape=[jax.ShapeDtypeStruct((b, lpad, D_GRP), F32), jax.ShapeDtypeStruct((b, D_GRP, D_GRP), F32)],
        scratch_shapes=[pltpu.VMEM((nb, D_GRP, D_GRP), F32), pltpu.VMEM((nb, 1, fd), F32)],
        compiler_params=_params(("parallel", "arbitrary")),
        name="rwkv",
    )(f, shift0.reshape(b, 1, fd), s0_bd, *plist)


def _s5_discretise(lre, lim, ldt):
    step = jnp.exp(ldt)
    mag = jnp.exp(lre * step)
    are, aim = mag * jnp.cos(lim * step), mag * jnp.sin(lim * step)
    den = lre * lre + lim * lim
    return are, aim, ((are - 1.0) * lre + aim * lim) / den, (aim * lre - (are - 1.0) * lim) / den


def _s5_gate(y, u, dsk, wglu, bglu, bf16):
    y = y + dsk * u
    y = 0.5 * y * (1.0 + jnp.tanh(math.sqrt(2.0 / math.pi) * (y + 0.044715 * (y * y * y))))
    return y * _sigmoid(_mm(y, wglu, bf16) + bglu)


def _s5_step_kernel(u_ref, h0r_ref, h0i_ref, lre_ref, lim_ref, ldt_ref, bblk_ref, cre_ref, cim_ref, dsk_ref,
                    wglu_ref, bglu_ref, y_ref, hr_out_ref, hi_out_ref, *, bf16):
    ns = h0r_ref.shape[1]
    are, aim, cr, ci = _s5_discretise(lre_ref[...], lim_ref[...], ldt_ref[...])
    u = u_ref[...]
    ub = _mm(u, bblk_ref[...], bf16)
    ubr, ubi = ub[:, :ns], ub[:, ns:]
    h0r, h0i = h0r_ref[...], h0i_ref[...]
    hre = are * h0r - aim * h0i + (cr * ubr - ci * ubi)
    him = are * h0i + aim * h0r + (cr * ubi + ci * ubr)
    hr_out_ref[...] = hre
    hi_out_ref[...] = him
    y = _mm(hre, cre_ref[...], bf16) - _mm(him, cim_ref[...], bf16)
    y_ref[...] = _s5_gate(y, u, dsk_ref[...], wglu_ref[...], bglu_ref[...], bf16)


def _s5_kernel(u_ref, h0r_ref, h0i_ref, lre_ref, lim_ref, ldt_ref, bblk_ref, cre_ref, cim_ref, dsk_ref,
               wglu_ref, bglu_ref, y_ref, hr_out_ref, hi_out_ref,
               ar_ref, ai_ref, cr_ref, ci_ref, pwr_ref, pwi_ref, hr_ref, hi_ref, sr_ref, si_ref, perm_ref,
               *, last_row, bf16):
    i = pl.program_id(1)
    tt = u_ref.shape[1]
    seg = tt // 8
    ns = ar_ref.shape[1]

    @pl.when(i == 0)
    def _():
        are, aim, cre0, cim0 = _s5_discretise(lre_ref[...], lim_ref[...], ldt_ref[...])
        cr_ref[...] = cre0
        ci_ref[...] = cim0
        ar_ref[...] = are
        ai_ref[...] = aim
        pwr_ref[0:8, :] = jnp.broadcast_to(are, (8, ns))
        pwi_ref[0:8, :] = jnp.broadcast_to(aim, (8, ns))
        n = 1
        while n < seg:
            tr, ti = pwr_ref[8 * n - 1:8 * n, :], pwi_ref[8 * n - 1:8 * n, :]
            xr, xi = pwr_ref[0:8 * n, :], pwi_ref[0:8 * n, :]
            pwr_ref[8 * n:16 * n, :] = xr * tr - xi * ti
            pwi_ref[8 * n:16 * n, :] = xr * ti + xi * tr
            n *= 2
        hr_ref[...] = h0r_ref[0]
        hi_ref[...] = h0i_ref[0]
        if seg > 1:
            rr = lax.broadcasted_iota(jnp.int32, (tt, tt), 0)
            cc = lax.broadcasted_iota(jnp.int32, (tt, tt), 1)
            perm_ref[...] = jnp.where((rr % 8) * seg + rr // 8 == cc, 1.0, 0.0).astype(BF16)

    def permute(x, inverse):
        if seg == 1:
            return x
        hi = x.astype(BF16)
        lo = (x - hi.astype(F32)).astype(BF16)
        pm = perm_ref[...]
        dn = (((0,), (0,)), ((), ())) if inverse else (((1,), (0,)), ((), ()))
        return (lax.dot_general(pm, hi, dn, preferred_element_type=F32)
                + lax.dot_general(pm, lo, dn, preferred_element_type=F32))

    u = permute(u_ref[0], False)
    ub = _mm(u, bblk_ref[...], bf16)
    ubr, ubi = ub[:, :ns], ub[:, ns:]
    cr, ci = cr_ref[...], ci_ref[...]
    sr_ref[...] = cr * ubr - ci * ubi
    si_ref[...] = cr * ubi + ci * ubr

    are, aim = ar_ref[...], ai_ref[...]

    def local(j, carry):
        hr, hi = carry
        r0 = pl.multiple_of(j * 8, 8)
        nr = are * hr - aim * hi + sr_ref[pl.ds(r0, 8), :]
        ni = are * hi + aim * hr + si_ref[pl.ds(r0, 8), :]
        sr_ref[pl.ds(r0, 8), :] = nr
        si_ref[pl.ds(r0, 8), :] = ni
        return nr, ni

    z8 = jnp.zeros((8, ns), F32)
    er, ei = lax.fori_loop(0, seg, local, (z8, z8))

    pr_end, pi_end = pwr_ref[tt - 1:tt, :], pwi_ref[tt - 1:tt, :]
    cr_h, ci_h = hr_ref[...], hi_ref[...]
    ent_r, ent_i = [], []
    for s in range(8):
        ent_r.append(cr_h)
        ent_i.append(ci_h)
        cr_h, ci_h = (er[s:s + 1] + pr_end * cr_h - pi_end * ci_h,
                      ei[s:s + 1] + pr_end * ci_h + pi_end * cr_h)
    hr_ref[...] = cr_h
    hi_ref[...] = ci_h
    rep = lambda rows: jnp.broadcast_to(jnp.concatenate(rows, axis=0)[None], (seg, 8, ns)).reshape(tt, ns)
    ent_r, ent_i = rep(ent_r), rep(ent_i)
    pr, pi = pwr_ref[...], pwi_ref[...]
    hre = sr_ref[...] + pr * ent_r - pi * ent_i
    him = si_ref[...] + pr * ent_i + pi * ent_r

    y = _mm(hre, cre_ref[...], bf16) - _mm(him, cim_ref[...], bf16)
    y_ref[0] = permute(_s5_gate(y, u, dsk_ref[...], wglu_ref[...], bglu_ref[...], bf16), True)

    @pl.when(i == pl.num_programs(1) - 1)
    def _():
        hr_out_ref[0] = hre[last_row:last_row + 1, :]
        hi_out_ref[0] = him[last_row:last_row + 1, :]


def _s5(u, h0r, h0i, lp, valid_len, bf16):
    b, lpad, d = u.shape
    g, n = lp['lam_re'].shape
    p = SSM_GROUP
    ns = g * n
    tt = _row_tile(lpad, 512)
    eye_g = jnp.eye(g, dtype=F32)
    bre = jnp.einsum('gnp,gh->gphn', lp['b_re'], eye_g).reshape(g * p, ns)
    bim = jnp.einsum('gnp,gh->gphn', lp['b_im'], eye_g).reshape(g * p, ns)
    bblk = jnp.concatenate([bre, bim], axis=1)
    cre = jnp.einsum('gpn,gh->gnhp', lp['c_re'], eye_g).reshape(ns, g * p)
    cim = jnp.einsum('gpn,gh->gnhp', lp['c_im'], eye_g).reshape(ns, g * p)
    ldt = jnp.repeat(lp['log_dt'], n).reshape(1, ns)
    plist = [lp['lam_re'].reshape(1, ns), lp['lam_im'].reshape(1, ns), ldt, bblk, cre, cim,
             lp['d_skip'].reshape(1, d), lp['w_glu'], lp['b_glu'].reshape(1, d)]
    if lpad == 1:
        whole = lambda a: pl.BlockSpec(a.shape, lambda: (0,) * a.ndim)
        args = [u.reshape(b, d), h0r.reshape(b, ns), h0i.reshape(b, ns)] + plist
        y, hr, hi = pl.pallas_call(
            functools.partial(_s5_step_kernel, bf16=bf16),
            in_specs=[whole(a) for a in args],
            out_specs=[whole(jax.ShapeDtypeStruct((b, w), F32)) for w in (d, ns, ns)],
            out_shape=[jax.ShapeDtypeStruct((b, w), F32) for w in (d, ns, ns)],
            compiler_params=pltpu.CompilerParams(vmem_limit_bytes=VMEM_LIMIT),
            name="s5_step",
        )(*args)
        return y.reshape(b, 1, d), hr.reshape(b, 1, ns), hi.reshape(b, 1, ns)
    pspec = lambda a: pl.BlockSpec(a.shape, lambda bi, ti: (0,) * a.ndim)
    st_spec = pl.BlockSpec((1, 1, ns), lambda bi, ti: (bi, 0, 0))
    seg = tt // 8
    t_last = (valid_len - 1) % tt
    vm = lambda r: pltpu.VMEM((r, ns), F32)
    return pl.pallas_call(
        functools.partial(_s5_kernel, last_row=8 * (t_last % seg) + t_last // seg, bf16=bf16),
        grid=(b, lpad // tt),
        in_specs=[pl.BlockSpec((1, tt, d), lambda bi, ti: (bi, ti, 0)), st_spec, st_spec]
                 + [pspec(a) for a in plist],
        out_specs=[pl.BlockSpec((1, tt, d), lambda bi, ti: (bi, ti, 0)), st_spec, st_spec],
        out_shape=[jax.ShapeDtypeStruct((b, lpad, d), F32), jax.ShapeDtypeStruct((b, 1, ns), F32),
                   jax.ShapeDtypeStruct((b, 1, ns), F32)],
        scratch_shapes=[vm(1), vm(1), vm(1), vm(1), vm(tt), vm(tt), vm(1), vm(1), vm(tt), vm(tt),
                        pltpu.VMEM((tt, tt), BF16)],
        compiler_params=_params(("arbitrary", "arbitrary")),
        name="s5",
    )(u, h0r, h0i, *plist)


def _pool_kernel(u_ref, hist_ref, wp_ref, sc_ref, o_ref, ext_ref, *, pos0):
    i = pl.program_id(1)
    tt = u_ref.shape[1]
    hpad = 16

    @pl.when(i == 0)
    def _():
        ext_ref[0:hpad, :] = hist_ref[0]

    u = u_ref[0]
    ext_ref[hpad:hpad + tt, :] = u
    grp = lax.broadcasted_iota(jnp.int32, (1, D_GRP), 1) // (D_GRP // len(POOL_WINDOWS))
    pos = pos0 + i * tt + lax.broadcasted_iota(jnp.int32, (tt, 1), 0)
    acc = u
    pooled = jnp.zeros_like(u)
    d = 1
    for gi, w in enumerate(POOL_WINDOWS):
        while d < w:
            acc = acc + ext_ref[hpad - d:hpad - d + tt, :]
            d += 1
        cnt = jnp.minimum(pos + 1, w).astype(F32)
        pooled = jnp.where(grp == gi, acc / cnt, pooled)
    pooled = pooled - u
    o_ref[0] = _mm(pooled, wp_ref[...], False) * sc_ref[...]
    ext_ref[0:hpad, :] = ext_ref[tt:tt + hpad, :]


def _pool(u, hist16, lp, pos0):
    b, lpad, d = u.shape
    tt = _row_tile(lpad, 512)
    nw = len(POOL_WINDOWS)
    gw = d // nw
    wp = jnp.einsum('gcd,gh->gchd', lp['w_pool'], jnp.eye(nw, dtype=F32)).reshape(d, d)
    return pl.pallas_call(
        functools.partial(_pool_kernel, pos0=pos0),
        grid=(b, lpad // tt),
        in_specs=[pl.BlockSpec((1, tt, d), lambda bi, ti: (bi, ti, 0)),
                  pl.BlockSpec((1, 16, d), lambda bi, ti: (bi, 0, 0)),
                  pl.BlockSpec((d, d), lambda bi, ti: (0, 0)),
                  pl.BlockSpec((1, d), lambda bi, ti: (0, 0))],
        out_specs=pl.BlockSpec((1, tt, d), lambda bi, ti: (bi, ti, 0)),
        out_shape=jax.ShapeDtypeStruct((b, lpad, d), F32),
        scratch_shapes=[pltpu.VMEM((tt + 16, d), F32)],
        compiler_params=_params(("parallel", "arbitrary")),
        name="pool",
    )(u, hist16, wp, lp['pool_scale'].reshape(1, d))


def _bd_from_heads(s):
    b = s.shape[0]
    return jnp.einsum('bhvk,hg->bhvgk', s, jnp.eye(N_HEADS, dtype=s.dtype)).reshape(b, D_GRP, D_GRP)


def _heads_from_bd(s):
    b = s.shape[0]
    s5 = s.reshape(b, N_HEADS, HEAD_DIM, N_HEADS, HEAD_DIM)
    return jnp.stack([s5[:, h, :, h, :] for h in range(N_HEADS)], axis=1)


def _pad_rows(a, lpad):
    return a if a.shape[1] == lpad else jnp.pad(a, ((0, 0), (0, lpad - a.shape[1]), (0, 0)))


def _run_trunk(x, pos0, cache_k, cache_v, page_table, wkv0, shift0, ssm_re0, ssm_im0, pool0, P, final_g,
               bf16_small):
    bsz, l, d = x.shape
    depth = P['w_in'].shape[0]
    n = bsz * l
    lpad = -(-l // 8) * 8
    fd = shift0.shape[-1]
    widths = (D_GRP, D_GRP, D_GRP, fd, D_GRP, D_GRP)
    decode = cache_k is not None
    wdt = F32 if decode else BF16
    big = {name: P[name].astype(wdt) for name in ('w_in', 'w_out', 'w_gate', 'w_up', 'w_down')}
    if decode:
        page_size = cache_k.shape[2]
        assert (page_table.shape[1] * page_size) % MOBA_BLOCK == 0 and page_table.shape[1] * page_size >= MOBA_TOPK * MOBA_BLOCK
        ck = jnp.transpose(cache_k, (0, 1, 3, 4, 2))
        cv = jnp.transpose(cache_v, (0, 1, 3, 4, 2))
    else:
        kv_t = (jnp.zeros((depth, bsz, D_GRP, l), F32),) * 2
    xf = x.reshape(n, d)
    outs = []
    for li in range(depth):
        lp = {name: arr[li] for name, arr in P.items()}
        if decode:
            q, k, v, f, us, up = _inproj(xf, P['norm1_g'], big['w_in'], li, widths)
            top = _select_blocks(page_table, q, ck, li)
            att = _attn_decode(top, page_table, q, k, v, ck, cv, li)
            kv_new = (k.reshape(bsz, l, N_HEADS, HEAD_DIM), v.reshape(bsz, l, N_HEADS, HEAD_DIM))
        else:
            q, f, us, up, kb, vb, km, *kv_t = _inproj(xf, P['norm1_g'], big['w_in'], li, widths, kv_t)
            kv_new = ()
            att = _attn_prompt(q.reshape(bsz, l, D_GRP), kb.reshape(bsz, l, D_GRP), vb.reshape(bsz, l, D_GRP),
                               km.reshape(bsz, l // MOBA_BLOCK, D_GRP)).reshape(n, D_GRP)
        f3 = f.reshape(bsz, l, fd)
        rw, s_new = _rwkv(_pad_rows(f3, lpad), shift0[li], _bd_from_heads(wkv0[li]), lp, l, bf16_small)
        g_n = ssm_re0.shape[2] * ssm_re0.shape[3]
        us3 = us.reshape(bsz, l, D_GRP)
        ss, hr, hi = _s5(us3 if l == 1 else _pad_rows(us3, lpad), ssm_re0[li].reshape(bsz, 1, g_n),
                         ssm_im0[li].reshape(bsz, 1, g_n), lp, l, bf16_small)
        up3 = up.reshape(bsz, l, D_GRP)
        hist16 = jnp.pad(pool0[li], ((0, 0), (1, 0), (0, 0)))
        pm = _pool(_pad_rows(up3, lpad), hist16, lp, pos0)
        xf, = _post(att, rw[:, :l].reshape(n, D_GRP), ss[:, :l].reshape(n, D_GRP),
                    pm[:, :l].reshape(n, D_GRP), xf, P['mix_g'], big['w_out'], P['norm2_g'],
                    big['w_gate'], big['w_up'], big['w_down'], final_g, li, li == depth - 1)
        pool_new = jnp.concatenate([pool0[li], up3], axis=1)[:, -POOL_HIST:]
        outs.append(kv_new + (_heads_from_bd(s_new), f3[:, -1], hr.reshape(ssm_re0.shape[1:]),
                              hi.reshape(ssm_im0.shape[1:]), pool_new))
    new_state = [jnp.stack(s, axis=0) for s in zip(*outs)]
    if not decode:
        new_state = [t.reshape(depth, bsz, N_HEADS, HEAD_DIM, l).transpose(0, 1, 4, 2, 3) for t in kv_t] + new_state
    return xf.reshape(bsz, l, d), new_state


def kernel(x_prompt, x_sample, cache_k, cache_v, page_table, state_wkv, state_shift, state_ssm_re, state_ssm_im, state_pool, norm1_g, w_in, mu_shift, w0, w2, a0, a2, g2, k_k, k_a, r_k, lnx_g, lnx_b, lam_re, lam_im, log_dt, b_re, b_im, c_re, c_im, d_skip, w_glu, b_glu, w_pool, pool_scale, mix_g, w_out, norm2_g, w_gate, w_up, w_down, final_g):
    P = dict(norm1_g=norm1_g, w_in=w_in, mu_shift=mu_shift, w0=w0, w2=w2, a0=a0, a2=a2, g2=g2,
             k_k=k_k, k_a=k_a, r_k=r_k, lnx_g=lnx_g, lnx_b=lnx_b, lam_re=lam_re, lam_im=lam_im,
             log_dt=log_dt, b_re=b_re, b_im=b_im, c_re=c_re, c_im=c_im, d_skip=d_skip, w_glu=w_glu,
             b_glu=b_glu, w_pool=w_pool, pool_scale=pool_scale, mix_g=mix_g, w_out=w_out,
             norm2_g=norm2_g, w_gate=w_gate, w_up=w_up, w_down=w_down)
    dt = x_prompt.dtype
    bp = x_prompt.shape[0]
    depth = w_in.shape[0]
    z = lambda *s: jnp.zeros((depth, bp) + s, dt)
    y_p, (k_p, v_p, wkv_p, shift_p, sre_p, sim_p, pool_p) = _run_trunk(
        x_prompt, 0, None, None, None, z(*state_wkv.shape[2:]), z(state_shift.shape[2]),
        z(*state_ssm_re.shape[2:]), z(*state_ssm_im.shape[2:]), z(*state_pool.shape[2:]), P, final_g, True)
    past_len = page_table.shape[1] * cache_k.shape[2]
    y_s, (k_s, v_s, wkv_s, shift_s, sre_s, sim_s, pool_s) = _run_trunk(
        x_sample, past_len, cache_k, cache_v, page_table, state_wkv, state_shift, state_ssm_re,
        state_ssm_im, state_pool, P, final_g, False)
    return (y_p, y_s, k_p, v_p, k_s, v_s, wkv_p, wkv_s, shift_p, shift_s,
            sre_p, sim_p, sre_s, sim_s, pool_p, pool_s)
```

```python
import functools
import math

import jax
import jax.numpy as jnp
from jax import lax
from jax.experimental import pallas as pl
from jax.experimental.pallas import tpu as pltpu

F32 = jnp.float32
BF16 = jnp.bfloat16
HI = lax.Precision.HIGHEST

HEAD_DIM = 64
N_HEADS = 4
D_GRP = N_HEADS * HEAD_DIM
MOBA_BLOCK = 256
MOBA_TOPK = 3
RWKV_LN_EPS = 64e-5
RMS_EPS = 1e-6
NEG = -1e30
SSM_GROUP = 16
SSM_STATE = 64
POOL_WINDOWS = (2, 4, 8, 16)
POOL_HIST = 15
ALIBI_SLOPES = tuple(2.0 ** (-8.0 * (h + 1) / N_HEADS) for h in range(N_HEADS))
VMEM_LIMIT = 56 * 1024 * 1024


def _params(sem):
    return pltpu.CompilerParams(dimension_semantics=sem, vmem_limit_bytes=VMEM_LIMIT)


def _mm(a, b, bf16):
    if bf16:
        return jnp.dot(a.astype(BF16), b.astype(BF16), preferred_element_type=F32)
    return jnp.dot(a, b, precision=HI, preferred_element_type=F32)


def _mm_nt(a, b, bf16):
    dn = (((1,), (1,)), ((), ()))
    if bf16:
        return lax.dot_general(a.astype(BF16), b.astype(BF16), dn, preferred_element_type=F32)
    return lax.dot_general(a, b, dn, precision=HI, preferred_element_type=F32)


def _act(x, w_ref):
    return x.astype(BF16) if w_ref.dtype == BF16 else x


def _dot_w(a, w):
    if w.dtype == BF16:
        return jnp.dot(a, w, preferred_element_type=F32)
    return jnp.dot(a, w, precision=HI, preferred_element_type=F32)


def _mm_pieces(a, b_exact, n):
    bb = b_exact.astype(BF16)
    out, rest = None, a
    for _ in range(n):
        piece = rest.astype(BF16)
        rest = rest - piece.astype(F32)
        d = jnp.dot(piece, bb, preferred_element_type=F32)
        out = d if out is None else out + d
    return out


def _mm_pieces_l(a_exact, b, n):
    ab = a_exact.astype(BF16)
    out, rest = None, b
    for _ in range(n):
        piece = rest.astype(BF16)
        rest = rest - piece.astype(F32)
        d = jnp.dot(ab, piece, preferred_element_type=F32)
        out = d if out is None else out + d
    return out


def _mm_split(a, b):
    ah, bh = a.astype(BF16), b.astype(BF16)
    al, bl = (a - ah.astype(F32)).astype(BF16), (b - bh.astype(F32)).astype(BF16)
    dot = lambda x, y: jnp.dot(x, y, preferred_element_type=F32)
    return dot(ah, bh) + (dot(ah, bl) + dot(al, bh))


def _mm_nt_split(a, b):
    dn = (((1,), (1,)), ((), ()))
    ah, bh = a.astype(BF16), b.astype(BF16)
    al, bl = (a - ah.astype(F32)).astype(BF16), (b - bh.astype(F32)).astype(BF16)
    dot = lambda x, y: lax.dot_general(x, y, dn, preferred_element_type=F32)
    return dot(ah, bh) + (dot(ah, bl) + dot(al, bh))


def _rms(x, g):
    return x * lax.rsqrt(jnp.mean(x * x, axis=-1, keepdims=True) + RMS_EPS) * g


def _sigmoid(x):
    return 1.0 / (1.0 + jnp.exp(-x))


def _row_tile(n, target):
    t = min(n, target)
    while n % t:
        t //= 2
    return t


def _inproj_kernel(x_ref, g_ref, w_ref, *refs, widths, prompt):
    h = _act(_rms(x_ref[...], g_ref[0]), w_ref)
    c0 = 0
    zs = []
    for wd in widths:
        zs.append(_dot_w(h, w_ref[0, :, c0:c0 + wd]))
        c0 += wd
    if not prompt:
        for o_ref, z in zip(refs, zs):
            o_ref[...] = z
        return
    q_ref, f_ref, us_ref, up_ref, kb_ref, vb_ref, km_ref, kt_ref, vt_ref = refs[2:]
    q_ref[...] = zs[0]
    f_ref[...] = zs[3]
    us_ref[...] = zs[4]
    up_ref[...] = zs[5]
    kb_ref[...] = zs[1].astype(BF16)
    vb_ref[...] = zs[2].astype(BF16)
    for j in range(km_ref.shape[1]):
        km_ref[0, j:j + 1, :] = jnp.mean(zs[1][j * MOBA_BLOCK:(j + 1) * MOBA_BLOCK], axis=0, keepdims=True)
    kt_ref[0, 0] = zs[1].T
    vt_ref[0, 0] = zs[2].T


def _inproj(x, g, w, layer, widths, kv_t=None):
    n, d = x.shape
    tm = _row_tile(n, 512)
    row = lambda wd, dt=F32: (pl.BlockSpec((tm, wd), lambda i: (i, 0)), jax.ShapeDtypeStruct((n, wd), dt))
    in_specs = [pl.BlockSpec((tm, d), lambda i: (i, 0)),
                pl.BlockSpec((1, 1, d), lambda i: (layer, 0, 0)),
                pl.BlockSpec((1,) + w.shape[1:], lambda i: (layer, 0, 0), pipeline_mode=pl.Buffered(1))]
    args = [x, g.reshape(g.shape[0], 1, d), w]
    aliases = {}
    if kv_t is None:
        outs = [row(wd) for wd in widths]
    else:
        seq = kv_t[0].shape[3]
        assert tm % MOBA_BLOCK == 0 and seq % tm == 0
        bpt, tps = tm // MOBA_BLOCK, seq // tm
        kv_spec = pl.BlockSpec((1, 1, D_GRP, tm), lambda i: (layer, i // tps, 0, i % tps))
        outs = [row(widths[0]), row(widths[3]), row(widths[4]), row(widths[5]), row(D_GRP, BF16), row(D_GRP, BF16),
                (pl.BlockSpec((1, bpt, D_GRP), lambda i: (i, 0, 0)), jax.ShapeDtypeStruct((n // tm, bpt, D_GRP), F32)),
                (kv_spec, jax.ShapeDtypeStruct(kv_t[0].shape, F32)), (kv_spec, jax.ShapeDtypeStruct(kv_t[1].shape, F32))]
        in_specs += [pl.BlockSpec(memory_space=pl.ANY)] * 2
        args += list(kv_t)
        aliases = {3: len(outs) - 2, 4: len(outs) - 1}
    return pl.pallas_call(
        functools.partial(_inproj_kernel, widths=widths, prompt=kv_t is not None),
        grid=(n // tm,),
        in_specs=in_specs,
        out_specs=[o[0] for o in outs],
        out_shape=[o[1] for o in outs],
        input_output_aliases=aliases,
        compiler_params=_params(("parallel",)),
        name="inproj",
    )(*args)


def _post_kernel(att_ref, rw_ref, ss_ref, pm_ref, x_ref, mixg_ref, wout_ref, n2g_ref,
                 wg_ref, wu_ref, wd_ref, fing_ref, *out_refs, final):
    acc = x_ref[...]
    for i, ref in enumerate((att_ref, rw_ref, ss_ref, pm_ref)):
        c = slice(i * D_GRP, (i + 1) * D_GRP)
        acc = acc + _dot_w(_act(_rms(ref[...], mixg_ref[0, :, c]), wout_ref), wout_ref[0, c, :])
    h2 = _act(_rms(acc, n2g_ref[0]), wg_ref)
    gt = _dot_w(h2, wg_ref[0])
    up = _dot_w(h2, wu_ref[0])
    acc = acc + _dot_w(_act(gt * _sigmoid(gt) * up, wd_ref), wd_ref[0])
    out_refs[0][...] = _rms(acc, fing_ref[...]) if final else acc


def _post(att, rw, ss, pm, x, mix_g, wout, n2g, wg, wu, wd, fin_g, layer, final):
    n, d = x.shape
    tm = _row_tile(n, 512)
    row = lambda w: pl.BlockSpec((tm, w), lambda i: (i, 0))
    full = lambda a: pl.BlockSpec((1,) + a.shape[1:], lambda i: (layer, 0, 0), pipeline_mode=pl.Buffered(1))
    vec = lambda: pl.BlockSpec((1, 1, d), lambda i: (layer, 0, 0))
    v3 = lambda a: a.reshape(a.shape[0], 1, d)
    return pl.pallas_call(
        functools.partial(_post_kernel, final=final),
        grid=(n // tm,),
        in_specs=[row(D_GRP), row(D_GRP), row(D_GRP), row(D_GRP), row(d), vec(), full(wout), vec(),
                  full(wg), full(wu), full(wd), pl.BlockSpec((1, d), lambda i: (0, 0))],
        out_specs=[row(d)],
        out_shape=[jax.ShapeDtypeStruct((n, d), F32)],
        compiler_params=_params(("parallel",)),
        name="post",
    )(att, rw, ss, pm, x, v3(mix_g), wout, v3(n2g), wg, wu, wd, fin_g.reshape(1, d))


def _attn_kernel(q_ref, kb_ref, vb_ref, km_ref, o_ref, *, nblk):
    qi = pl.program_id(1)
    blk = MOBA_BLOCK
    hd2 = 2 * HEAD_DIM
    n_pairs = N_HEADS // 2
    f_row, f_col, f_inv = nblk, nblk + 1, nblk + 2
    lane = lax.broadcasted_iota(jnp.int32, (blk, hd2), 1)
    rowf = lax.broadcasted_iota(jnp.int32, (blk, hd2), 0).astype(F32)
    causal = (lax.broadcasted_iota(jnp.int32, (blk, blk), 0) >= lax.broadcasted_iota(jnp.int32, (blk, blk), 1))
    bi = lax.broadcasted_iota(jnp.int32, (nblk, blk), 0)
    bif = bi.astype(F32)
    km_lane = lax.broadcasted_iota(jnp.int32, (nblk, hd2), 1)
    k_static = jnp.where(lane == f_row, 1.0, jnp.where(lane == f_col, rowf, 0.0))

    def key_feat(j, fully_past):
        return jnp.where(lane == jnp.where(fully_past, j, f_inv), 1.0, k_static).astype(BF16)

    gates = []
    for hp in range(n_pairs):
        cs = slice(hp * hd2, (hp + 1) * hd2)
        km2 = km_ref[0, :, cs]
        km_heads = jnp.concatenate([jnp.where((km_lane // HEAD_DIM) == hh, km2, 0.0) for hh in range(2)], axis=0)
        gates.append(_mm_nt_split(km_heads, q_ref[0, :, cs]))

    qps = []
    for h in range(N_HEADS):
        cs = slice((h // 2) * hd2, (h // 2 + 1) * hd2)
        q2 = q_ref[0, :, cs]
        slope = ALIBI_SLOPES[h]
        gate = gates[h // 2][(h % 2) * nblk:(h % 2 + 1) * nblk]
        g = jnp.where(bi < qi, gate, NEG)
        sel = bi == qi
        for r in range(MOBA_TOPK):
            m = jnp.max(g, axis=0, keepdims=True)
            idx = jnp.min(jnp.where(g == m, bif, float(nblk)), axis=0, keepdims=True)
            hit = bif == idx
            sel = sel | (hit & (qi > r))
            g = jnp.where(hit, -jnp.inf, g)
        rt = jnp.where(sel, (-slope * blk) * (qi - bi).astype(F32), NEG)
        rt = jnp.concatenate([rt, jnp.zeros((hd2 - nblk, blk), F32)], axis=0).T
        feat = jnp.where(lane == f_row, -slope * rowf,
                         jnp.where(lane == f_col, slope, jnp.where(lane == f_inv, NEG, rt)))
        qm = jnp.where((lane // HEAD_DIM) == h % 2, q2, 0.0) * (HEAD_DIM ** -0.5)
        qps.append(jnp.concatenate([qm.astype(BF16), feat.astype(BF16)], axis=1))

    r_own = pl.multiple_of(qi * blk, blk)
    own_feat = key_feat(qi, True)
    kps = [jnp.concatenate([kb_ref[0, pl.ds(r_own, blk), slice(hp * hd2, (hp + 1) * hd2)], own_feat], axis=1)
           for hp in range(n_pairs)]
    ss = [_mm_nt(qps[h], kps[h // 2], True) for h in range(N_HEADS)]
    stats, ps = [], []
    for h in range(N_HEADS):
        s = jnp.where(causal, ss[h], NEG)
        m0 = jnp.max(s, axis=-1, keepdims=True)
        p = jnp.exp(s - m0)
        stats.append((m0, jnp.sum(p, axis=-1, keepdims=True)))
        ps.append(p.astype(BF16))
    init = []
    for h in range(N_HEADS):
        cs = slice((h // 2) * hd2, (h // 2 + 1) * hd2)
        init.append(stats[h] + (jnp.dot(ps[h], vb_ref[0, pl.ds(r_own, blk), cs], preferred_element_type=F32),))

    def body(t, carry):
        r0 = pl.multiple_of(t * (2 * blk), 2 * blk)
        kfeat = jnp.concatenate([key_feat(2 * t, True), key_feat(2 * t + 1, 2 * t + 1 < qi)], axis=0)
        kps = [jnp.concatenate([kb_ref[0, pl.ds(r0, 2 * blk), slice(hp * hd2, (hp + 1) * hd2)], kfeat], axis=1)
               for hp in range(n_pairs)]
        ss = [_mm_nt(qps[h], kps[h // 2], True) for h in range(N_HEADS)]
        stats, ps = [], []
        for h in range(N_HEADS):
            m_i, l_i, _ = carry[h]
            m_new = jnp.maximum(m_i, jnp.max(ss[h], axis=-1, keepdims=True))
            p = jnp.exp(ss[h] - m_new)
            alpha = jnp.exp(m_i - m_new)
            stats.append((m_new, alpha * l_i + jnp.sum(p, axis=-1, keepdims=True), alpha))
            ps.append(p.astype(BF16))
        new = []
        for h in range(N_HEADS):
            cs = slice((h // 2) * hd2, (h // 2 + 1) * hd2)
            pv = jnp.dot(ps[h], vb_ref[0, pl.ds(r0, 2 * blk), cs], preferred_element_type=F32)
            new.append((stats[h][0], stats[h][1], stats[h][2] * carry[h][2] + pv))
        return tuple(new)

    res = lax.fori_loop(0, (qi + 1) // 2, body, tuple(init))
    for hp in range(n_pairs):
        o0 = res[2 * hp][2] / res[2 * hp][1]
        o1 = res[2 * hp + 1][2] / res[2 * hp + 1][1]
        o_ref[0, :, hp * hd2:(hp + 1) * hd2] = jnp.where((lane // HEAD_DIM) == 0, o0, o1)


def _attn_prompt(q, kb, vb, kmean):
    b, l, d = q.shape
    nblk = l // MOBA_BLOCK
    assert nblk + 3 <= 2 * HEAD_DIM
    return pl.pallas_call(
        functools.partial(_attn_kernel, nblk=nblk),
        grid=(b, nblk),
        in_specs=[pl.BlockSpec((1, MOBA_BLOCK, d), lambda i, j: (i, j, 0)),
                  pl.BlockSpec((1, l, d), lambda i, j: (i, 0, 0)),
                  pl.BlockSpec((1, l, d), lambda i, j: (i, 0, 0)),
                  pl.BlockSpec((1, nblk, d), lambda i, j: (i, 0, 0))],
        out_specs=pl.BlockSpec((1, MOBA_BLOCK, d), lambda i, j: (i, j, 0)),
        out_shape=jax.ShapeDtypeStruct((b, l, d), F32),
        compiler_params=_params(("parallel", "parallel")),
        name="attn_prompt",
    )(q, kb, vb, kmean)


def _gate_select_kernel(pt_ref, q_ref, *refs, pages_per_step, pages_per_blk, nblk):
    k_refs = refs[:pages_per_step]
    o_ref, g_ref = refs[pages_per_step], refs[pages_per_step + 1]
    s = pl.program_id(1)
    blocks_per_step = pages_per_step // pages_per_blk
    bi = lax.broadcasted_iota(jnp.int32, (nblk, 128), 0)
    lane = lax.broadcasted_iota(jnp.int32, (blocks_per_step, 128), 1)

    slab = jnp.zeros((blocks_per_step, 128), F32)
    for h in range(N_HEADS):
        qc = q_ref[0, h * HEAD_DIM:(h + 1) * HEAD_DIM, :]
        rows = []
        for jb in range(blocks_per_step):
            prod = k_refs[jb * pages_per_blk][0, 0, h] * qc
            for i in range(1, pages_per_blk):
                prod = prod + k_refs[jb * pages_per_blk + i][0, 0, h] * qc
            rows.append(jnp.sum(prod, axis=0, keepdims=True))
        val = jnp.sum(jnp.concatenate(rows, axis=0), axis=1, keepdims=True) * (1.0 / MOBA_BLOCK)
        slab = jnp.where(lane == h, val, slab)
    g_ref[pl.ds(pl.multiple_of(s * blocks_per_step, blocks_per_step), blocks_per_step), :] = slab

    @pl.when(s == pl.num_programs(1) - 1)
    def _():
        gg = g_ref[...]
        rows = []
        for r in range(MOBA_TOPK):
            m = jnp.max(gg, axis=0, keepdims=True)
            idx = jnp.min(jnp.where(gg == m, bi, nblk), axis=0, keepdims=True)
            rows.append(idx)
            gg = jnp.where(bi == idx, -jnp.inf, gg)
        rows.append(jnp.zeros((8 - MOBA_TOPK, 128), jnp.int32))
        o_ref[0] = jnp.concatenate(rows, axis=0)


def _select_blocks(page_table, q, cache_kt, layer):
    bsz, npg = page_table.shape
    page_size = cache_kt.shape[4]
    pages_per_blk = MOBA_BLOCK // page_size
    nblk = npg // pages_per_blk
    pages_per_step = pages_per_blk * _row_tile(nblk, 16)
    n_steps = npg // pages_per_step

    def page_spec(i):
        return pl.BlockSpec((1, 1, N_HEADS, HEAD_DIM, page_size),
                            lambda b, s, pt: (layer, pt[b, s * pages_per_step + i], 0, 0, 0))

    out = pl.pallas_call(
        functools.partial(_gate_select_kernel, pages_per_step=pages_per_step, pages_per_blk=pages_per_blk, nblk=nblk),
        grid_spec=pltpu.PrefetchScalarGridSpec(
            num_scalar_prefetch=1,
            grid=(bsz, n_steps),
            in_specs=[pl.BlockSpec((1, D_GRP, 1), lambda b, s, pt: (b, 0, 0))]
                     + [page_spec(i) for i in range(pages_per_step)],
            out_specs=pl.BlockSpec((1, 8, 128), lambda b, s, pt: (b, 0, 0)),
            scratch_shapes=[pltpu.VMEM((nblk, 128), F32)]),
        out_shape=jax.ShapeDtypeStruct((bsz, 8, 128), jnp.int32),
        compiler_params=_params(("arbitrary", "arbitrary")),
        name="select_blocks",
    )(page_table, q.reshape(bsz, D_GRP, 1), *([cache_kt] * pages_per_step))
    return out[:, :MOBA_TOPK, :N_HEADS]


def _decode_kernel(top_ref, pt_ref, q_ref, kn_ref, vn_ref, *refs, page_size, past_len):
    pages_per_blk = MOBA_BLOCK // page_size
    npg = MOBA_TOPK * pages_per_blk
    k_refs = refs[:N_HEADS * npg]
    v_refs = refs[N_HEADS * npg:2 * N_HEADS * npg]
    o_ref = refs[2 * N_HEADS * npg]
    b = pl.program_id(0)
    lane = lax.broadcasted_iota(jnp.int32, (1, page_size), 1)
    for h in range(N_HEADS):
        rs = slice(h * HEAD_DIM, (h + 1) * HEAD_DIM)
        qc = q_ref[0, rs, :] * (HEAD_DIM ** -0.5)
        s_own = jnp.sum(qc * kn_ref[0, rs, :], axis=0, keepdims=True)
        ss = []
        for i in range(npg):
            blk = top_ref[b, (i // pages_per_blk) * N_HEADS + h]
            dist = (past_len - blk * MOBA_BLOCK - (i % pages_per_blk) * page_size - lane).astype(F32)
            kt = k_refs[h * npg + i][0, 0, 0]
            ss.append(jnp.sum(kt * qc, axis=0, keepdims=True) - ALIBI_SLOPES[h] * dist)
        m = s_own
        for s in ss:
            m = jnp.maximum(m, jnp.max(s, axis=-1, keepdims=True))
        p_own = jnp.exp(s_own - m)
        l = p_own
        acc = jnp.zeros((HEAD_DIM, page_size), F32)
        for i in range(npg):
            p = jnp.exp(ss[i] - m)
            l = l + jnp.sum(p, axis=-1, keepdims=True)
            acc = acc + v_refs[h * npg + i][0, 0, 0] * p
        o_ref[0, rs, :] = (jnp.sum(acc, axis=-1, keepdims=True) + p_own * vn_ref[0, rs, :]) / l


def _attn_decode(top, page_table, q, k_new, v_new, cache_kt, cache_vt, layer):
    bsz, npg_seq = page_table.shape
    page_size = cache_kt.shape[4]
    pages_per_blk = MOBA_BLOCK // page_size
    npg = MOBA_TOPK * pages_per_blk
    past_len = npg_seq * page_size

    def page_spec(h, i):
        def imap(b, top_ref, pt_ref):
            blk = top_ref[b, (i // pages_per_blk) * N_HEADS + h]
            return (layer, pt_ref[b, blk * pages_per_blk + i % pages_per_blk], h, 0, 0)
        return pl.BlockSpec((1, 1, 1, HEAD_DIM, page_size), imap)

    col_spec = pl.BlockSpec((1, D_GRP, 1), lambda b, t, p: (b, 0, 0))
    col = lambda a: a.reshape(bsz, D_GRP, 1)
    page_specs = [page_spec(h, i) for h in range(N_HEADS) for i in range(npg)]
    out = pl.pallas_call(
        functools.partial(_decode_kernel, page_size=page_size, past_len=past_len),
        grid_spec=pltpu.PrefetchScalarGridSpec(
            num_scalar_prefetch=2,
            grid=(bsz,),
            in_specs=[col_spec, col_spec, col_spec] + page_specs * 2,
            out_specs=col_spec),
        out_shape=jax.ShapeDtypeStruct((bsz, D_GRP, 1), F32),
        compiler_params=_params(("arbitrary",)),
        name="attn_decode",
    )(top.reshape(bsz, MOBA_TOPK * N_HEADS), page_table, col(q), col(k_new), col(v_new),
      *([cache_kt] * (N_HEADS * npg)), *([cache_vt] * (N_HEADS * npg)))
    return out.reshape(bsz, D_GRP)


def _rwkv_kernel(f_ref, sh0_ref, s0_ref, mu_ref, w0_ref, w2_ref, a0_ref, a2_ref, g2_ref, kk_ref, ka_ref,
                 rk_ref, lg_ref, lb_ref, o_ref, s_out_ref, s_ref, prev_ref, *, chunk, valid_len, bf16):
    i = pl.program_id(1)
    nb, tt, fd = f_ref.shape
    n_chunks = tt // chunk
    ht = N_HEADS * chunk
    rows = nb * tt
    n_pc = 2 if bf16 else 3

    @pl.when(i == 0)
    def _():
        s_ref[...] = s0_ref[...]
        prev_ref[...] = sh0_ref[...]

    f = f_ref[...].reshape(rows, fd)
    row = lax.broadcasted_iota(jnp.int32, (rows, 1), 0)
    prev = pltpu.roll(f, 1, 0)
    for b in range(nb):
        prev = jnp.where(row == b * tt, prev_ref[b], prev)
        prev_ref[b] = f[(b + 1) * tt - 1:(b + 1) * tt, :]
    m = f + (prev - f) * mu_ref[...]
    r = m[:, 0:D_GRP]
    k = m[:, D_GRP:2 * D_GRP]
    v = m[:, 2 * D_GRP:3 * D_GRP]
    c0 = 3 * D_GRP
    dr = w2_ref.shape[0]
    ar = a2_ref.shape[0]
    w_lo = m[:, c0:c0 + dr]
    a_lo = m[:, c0 + dr:c0 + dr + ar]
    g_lo = m[:, c0 + dr + ar:]
    y = -(w0_ref[...] + (_mm_split(jnp.tanh(w_lo), w2_ref[...]) if bf16 else _mm(jnp.tanh(w_lo), w2_ref[...], False)))
    softplus = jnp.maximum(y, 0.0) + jnp.log1p(jnp.exp(-jnp.abs(y)))
    logw = -jnp.exp(-softplus - 0.5)
    a = _sigmoid(a0_ref[...] + (_mm_split(a_lo, a2_ref[...]) if bf16 else _mm(a_lo, a2_ref[...], False)))
    g = _mm(_sigmoid(g_lo), g2_ref[...], bf16)

    lane_h = lax.broadcasted_iota(jnp.int32, (1, D_GRP), 1) // HEAD_DIM
    same_head = ((lax.broadcasted_iota(jnp.int32, (D_GRP, D_GRP), 0) // HEAD_DIM)
                 == (lax.broadcasted_iota(jnp.int32, (D_GRP, D_GRP), 1) // HEAD_DIM))
    e_head = same_head.astype(F32)
    kk = k * kk_ref[...]
    kkn = kk * lax.rsqrt(jnp.maximum(_mm_pieces(kk * kk, e_head, n_pc), 1e-12))
    k2 = k * (1.0 + (a - 1.0) * ka_ref[...])
    bb = kkn * a
    if valid_len is not None:
        ok = (i * tt + row % tt) < valid_len
        logw = jnp.where(ok, logw, 0.0)
        kkn = jnp.where(ok, kkn, 0.0)
        k2m = jnp.where(ok, k2, 0.0)
        bb = jnp.where(ok, bb, 0.0)
        vm = jnp.where(ok, v, 0.0)
    else:
        k2m, vm = k2, v

    ri = lax.broadcasted_iota(jnp.int32, (ht, ht), 0)
    ci = lax.broadcasted_iota(jnp.int32, (ht, ht), 1)
    same_blk = (ri // chunk) == (ci // chunk)
    strict = same_blk & ((ri % chunk) > (ci % chunk))
    incl = same_blk & ((ri % chunk) >= (ci % chunk))
    eye = (ri == ci).astype(F32)
    tril = (lax.broadcasted_iota(jnp.int32, (chunk, chunk), 0)
            >= lax.broadcasted_iota(jnp.int32, (chunk, chunk), 1)).astype(F32)

    def stack(x):
        return jnp.concatenate([jnp.where(lane_h == h, x, 0.0) for h in range(N_HEADS)], axis=0)

    def tile(x):
        return jnp.concatenate([x] * N_HEADS, axis=0)

    def unstack(x):
        out = jnp.zeros((chunk, D_GRP), F32)
        for h in range(N_HEADS):
            out = jnp.where(lane_h == h, x[h * chunk:(h + 1) * chunk], out)
        return out

    cat = lambda parts: parts[0] if len(parts) == 1 else jnp.concatenate(parts, axis=0)
    chunks = [[slice(b * tt + c * chunk, b * tt + (c + 1) * chunk) for b in range(nb)] for c in range(n_chunks)]
    flat = [rs for b in range(nb) for rs in (chunks[c][b] for c in range(n_chunks))]

    cum = cat([_mm_pieces_l(tril, logw[rs], 3) for rs in flat])
    cum_end = cat([jnp.broadcast_to(cum[rs.stop - 1:rs.stop], (chunk, D_GRP)) for rs in flat])
    inv_p = jnp.exp(-cum)
    to_end = jnp.exp(cum_end - cum)
    kap = kkn * jnp.exp(cum - logw)
    khat = k2m * inv_p
    bhat = bb * inv_p
    rhat = r * jnp.exp(cum)
    k_end = k2m * to_end
    b_end = bb * to_end
    every = [rs for step in chunks for rs in step]
    stk = {rs.start: (stack(kap[rs]), stack(rhat[rs]), stack(khat[rs]), stack(bhat[rs])) for rs in every}
    lmat = {rs.start: jnp.where(strict, _mm_nt(stk[rs.start][0], stk[rs.start][3], bf16), 0.0) for rs in every}
    akk = {rs.start: jnp.where(strict, _mm_nt(stk[rs.start][0], stk[rs.start][2], bf16), 0.0) for rs in every}
    ark = {rs.start: jnp.where(incl, _mm_nt(stk[rs.start][1], stk[rs.start][2], bf16), 0.0) for rs in every}
    arb = {rs.start: jnp.where(incl, _mm_nt(stk[rs.start][1], stk[rs.start][3], bf16), 0.0) for rs in every}
    v_t = {rs.start: tile(vm[rs]) for rs in every}
    akk_v = {key: _mm(akk[key], v_t[key], bf16) for key in akk}
    ark_v = {key: _mm(ark[key], v_t[key], bf16) for key in ark}
    minv = {key: eye - lmat[key] for key in lmat}
    lp = dict(lmat)
    for _ in range(int(math.log2(chunk)) - 1):
        lp = {key: _mm(lp[key], lp[key], bf16) for key in lp}
        minv = {key: _mm(minv[key], eye + lp[key], bf16) for key in minv}

    state = [s_ref[b] for b in range(nb)]
    o_parts = {}
    for step in chunks:
        keys = [rs.start for rs in step]
        kap_s = [_mm_nt(kap[rs], state[b], bf16) for b, rs in enumerate(step)]
        r_s = [_mm_nt(rhat[rs], state[b], bf16) for b, rs in enumerate(step)]
        u_t = [_mm(minv[key], tile(kap_s[b]) + akk_v[key], bf16) for b, key in enumerate(keys)]
        o_t = [tile(r_s[b]) + ark_v[key] - _mm(arb[key], u_t[b], bf16) for b, key in enumerate(keys)]
        upd = [_mm(vm[rs].T, k_end[rs], bf16) - _mm(unstack(u_t[b]).T, b_end[rs], bf16) for b, rs in enumerate(step)]
        for b, rs in enumerate(step):
            state[b] = state[b] * jnp.exp(cum[rs.stop - 1:rs.stop]) + jnp.where(same_head, upd[b], 0.0)
            o_parts[rs.start] = unstack(o_t[b])
    for b in range(nb):
        s_ref[b] = state[b]

    o = cat([o_parts[rs.start] for rs in flat])
    mean = _mm_pieces(o, e_head, n_pc) * (1.0 / HEAD_DIM)
    d = o - mean
    var = _mm_pieces(d * d, e_head, n_pc) * (1.0 / HEAD_DIM)
    on = d * lax.rsqrt(var + RWKV_LN_EPS) * lg_ref[...] + lb_ref[...]
    bonus = _mm_pieces(r * k2 * rk_ref[...], e_head, n_pc) * v
    o_ref[...] = ((on + bonus) * g).reshape(nb, tt, D_GRP)

    @pl.when(i == pl.num_programs(1) - 1)
    def _():
        s_out_ref[...] = s_ref[...]


def _rwkv(f, shift0, s0_bd, lp, valid_len, bf16):
    b, lpad, fd = f.shape
    tt = _row_tile(lpad, 128)
    chunk = min(64, tt)
    nb = _row_tile(b, max(1, 512 // tt))
    vec = lambda a: a.reshape(1, -1)
    pspec = lambda a: pl.BlockSpec(a.shape, lambda bi, ti: (0,) * a.ndim)
    plist = [vec(lp['mu_shift']), vec(lp['w0']), lp['w2'], vec(lp['a0']), lp['a2'], lp['g2'], vec(lp['k_k']),
             vec(lp['k_a']), vec(lp['r_k']), vec(lp['lnx_g']), vec(lp['lnx_b'])]
    return pl.pallas_call(
        functools.partial(_rwkv_kernel, chunk=chunk, valid_len=None if valid_len == lpad else valid_len, bf16=bf16),
        grid=(b // nb, lpad // tt),
        in_specs=[pl.BlockSpec((nb, tt, fd), lambda bi, ti: (bi, ti, 0)),
                  pl.BlockSpec((nb, 1, fd), lambda bi, ti: (bi, 0, 0)),
                  pl.BlockSpec((nb, D_GRP, D_GRP), lambda bi, ti: (bi, 0, 0))] + [pspec(a) for a in plist],
        out_specs=[pl.BlockSpec((nb, tt, D_GRP), lambda bi, ti: (bi, ti, 0)),
                   pl.BlockSpec((nb, D_GRP, D_GRP), lambda bi, ti: (bi, 0, 0))],
        out_shape=[jax.ShapeDtypeStruct((b, lpad, D_GRP), F32), jax.ShapeDtypeStruct((b, D_GRP, D_GRP), F32)],
        scratch_shapes=[pltpu.VMEM((nb, D_GRP, D_GRP), F32), pltpu.VMEM((nb, 1, fd), F32)],
        compiler_params=_params(("parallel", "arbitrary")),
        name="rwkv",
    )(f, shift0.reshape(b, 1, fd), s0_bd, *plist)


def _s5_discretise(lre, lim, ldt):
    step = jnp.exp(ldt)
    mag = jnp.exp(lre * step)
    are, aim = mag * jnp.cos(lim * step), mag * jnp.sin(lim * step)
    den = lre * lre + lim * lim
    return are, aim, ((are - 1.0) * lre + aim * lim) / den, (aim * lre - (are - 1.0) * lim) / den


def _s5_gate(y, u, dsk, wglu, bglu, bf16):
    y = y + dsk * u
    y = 0.5 * y * (1.0 + jnp.tanh(math.sqrt(2.0 / math.pi) * (y + 0.044715 * (y * y * y))))
    return y * _sigmoid(_mm(y, wglu, bf16) + bglu)


def _s5_step_kernel(u_ref, h0r_ref, h0i_ref, lre_ref, lim_ref, ldt_ref, bblk_ref, cre_ref, cim_ref, dsk_ref,
                    wglu_ref, bglu_ref, y_ref, hr_out_ref, hi_out_ref, *, bf16):
    ns = h0r_ref.shape[1]
    are, aim, cr, ci = _s5_discretise(lre_ref[...], lim_ref[...], ldt_ref[...])
    u = u_ref[...]
    ub = _mm(u, bblk_ref[...], bf16)
    ubr, ubi = ub[:, :ns], ub[:, ns:]
    h0r, h0i = h0r_ref[...], h0i_ref[...]
    hre = are * h0r - aim * h0i + (cr * ubr - ci * ubi)
    him = are * h0i + aim * h0r + (cr * ubi + ci * ubr)
    hr_out_ref[...] = hre
    hi_out_ref[...] = him
    y = _mm(hre, cre_ref[...], bf16) - _mm(him, cim_ref[...], bf16)
    y_ref[...] = _s5_gate(y, u, dsk_ref[...], wglu_ref[...], bglu_ref[...], bf16)


def _s5_kernel(u_ref, h0r_ref, h0i_ref, lre_ref, lim_ref, ldt_ref, bblk_ref, cre_ref, cim_ref, dsk_ref,
               wglu_ref, bglu_ref, y_ref, hr_out_ref, hi_out_ref,
               ar_ref, ai_ref, cr_ref, ci_ref, pwr_ref, pwi_ref, hr_ref, hi_ref, sr_ref, si_ref, perm_ref,
               *, last_row, bf16):
    i = pl.program_id(1)
    tt = u_ref.shape[1]
    seg = tt // 8
    ns = ar_ref.shape[1]

    @pl.when(i == 0)
    def _():
        are, aim, cre0, cim0 = _s5_discretise(lre_ref[...], lim_ref[...], ldt_ref[...])
        cr_ref[...] = cre0
        ci_ref[...] = cim0
        ar_ref[...] = are
        ai_ref[...] = aim
        pwr_ref[0:8, :] = jnp.broadcast_to(are, (8, ns))
        pwi_ref[0:8, :] = jnp.broadcast_to(aim, (8, ns))
        n = 1
        while n < seg:
            tr, ti = pwr_ref[8 * n - 1:8 * n, :], pwi_ref[8 * n - 1:8 * n, :]
            xr, xi = pwr_ref[0:8 * n, :], pwi_ref[0:8 * n, :]
            pwr_ref[8 * n:16 * n, :] = xr * tr - xi * ti
            pwi_ref[8 * n:16 * n, :] = xr * ti + xi * tr
            n *= 2
        hr_ref[...] = h0r_ref[0]
        hi_ref[...] = h0i_ref[0]
        if seg > 1:
            rr = lax.broadcasted_iota(jnp.int32, (tt, tt), 0)
            cc = lax.broadcasted_iota(jnp.int32, (tt, tt), 1)
            perm_ref[...] = jnp.where((rr % 8) * seg + rr // 8 == cc, 1.0, 0.0).astype(BF16)

    def permute(x, inverse):
        if seg == 1:
            return x
        hi = x.astype(BF16)
        lo = (x - hi.astype(F32)).astype(BF16)
        pm = perm_ref[...]
        dn = (((0,), (0,)), ((), ())) if inverse else (((1,), (0,)), ((), ()))
        return (lax.dot_general(pm, hi, dn, preferred_element_type=F32)
                + lax.dot_general(pm, lo, dn, preferred_element_type=F32))

    u = permute(u_ref[0], False)
    ub = _mm(u, bblk_ref[...], bf16)
    ubr, ubi = ub[:, :ns], ub[:, ns:]
    cr, ci = cr_ref[...], ci_ref[...]
    sr_ref[...] = cr * ubr - ci * ubi
    si_ref[...] = cr * ubi + ci * ubr

    are, aim = ar_ref[...], ai_ref[...]

    def local(j, carry):
        hr, hi = carry
        r0 = pl.multiple_of(j * 8, 8)
        nr = are * hr - aim * hi + sr_ref[pl.ds(r0, 8), :]
        ni = are * hi + aim * hr + si_ref[pl.ds(r0, 8), :]
        sr_ref[pl.ds(r0, 8), :] = nr
        si_ref[pl.ds(r0, 8), :] = ni
        return nr, ni

    z8 = jnp.zeros((8, ns), F32)
    er, ei = lax.fori_loop(0, seg, local, (z8, z8))

    pr_end, pi_end = pwr_ref[tt - 1:tt, :], pwi_ref[tt - 1:tt, :]
    cr_h, ci_h = hr_ref[...], hi_ref[...]
    ent_r, ent_i = [], []
    for s in range(8):
        ent_r.append(cr_h)
        ent_i.append(ci_h)
        cr_h, ci_h = (er[s:s + 1] + pr_end * cr_h - pi_end * ci_h,
                      ei[s:s + 1] + pr_end * ci_h + pi_end * cr_h)
    hr_ref[...] = cr_h
    hi_ref[...] = ci_h
    rep = lambda rows: jnp.broadcast_to(jnp.concatenate(rows, axis=0)[None], (seg, 8, ns)).reshape(tt, ns)
    ent_r, ent_i = rep(ent_r), rep(ent_i)
    pr, pi = pwr_ref[...], pwi_ref[...]
    hre = sr_ref[...] + pr * ent_r - pi * ent_i
    him = si_ref[...] + pr * ent_i + pi * ent_r

    y = _mm(hre, cre_ref[...], bf16) - _mm(him, cim_ref[...], bf16)
    y_ref[0] = permute(_s5_gate(y, u, dsk_ref[...], wglu_ref[...], bglu_ref[...], bf16), True)

    @pl.when(i == pl.num_programs(1) - 1)
    def _():
        hr_out_ref[0] = hre[last_row:last_row + 1, :]
        hi_out_ref[0] = him[last_row:last_row + 1, :]


def _s5(u, h0r, h0i, lp, valid_len, bf16):
    b, lpad, d = u.shape
    g, n = lp['lam_re'].shape
    p = SSM_GROUP
    ns = g * n
    tt = _row_tile(lpad, 512)
    eye_g = jnp.eye(g, dtype=F32)
    bre = jnp.einsum('gnp,gh->gphn', lp['b_re'], eye_g).reshape(g * p, ns)
    bim = jnp.einsum('gnp,gh->gphn', lp['b_im'], eye_g).reshape(g * p, ns)
    bblk = jnp.concatenate([bre, bim], axis=1)
    cre = jnp.einsum('gpn,gh->gnhp', lp['c_re'], eye_g).reshape(ns, g * p)
    cim = jnp.einsum('gpn,gh->gnhp', lp['c_im'], eye_g).reshape(ns, g * p)
    ldt = jnp.repeat(lp['log_dt'], n).reshape(1, ns)
    plist = [lp['lam_re'].reshape(1, ns), lp['lam_im'].reshape(1, ns), ldt, bblk, cre, cim,
             lp['d_skip'].reshape(1, d), lp['w_glu'], lp['b_glu'].reshape(1, d)]
    if lpad == 1:
        whole = lambda a: pl.BlockSpec(a.shape, lambda: (0,) * a.ndim)
        args = [u.reshape(b, d), h0r.reshape(b, ns), h0i.reshape(b, ns)] + plist
        y, hr, hi = pl.pallas_call(
            functools.partial(_s5_step_kernel, bf16=bf16),
            in_specs=[whole(a) for a in args],
            out_specs=[whole(jax.ShapeDtypeStruct((b, w), F32)) for w in (d, ns, ns)],
            out_shape=[jax.ShapeDtypeStruct((b, w), F32) for w in (d, ns, ns)],
            compiler_params=pltpu.CompilerParams(vmem_limit_bytes=VMEM_LIMIT),
            name="s5_step",
        )(*args)
        return y.reshape(b, 1, d), hr.reshape(b, 1, ns), hi.reshape(b, 1, ns)
    pspec = lambda a: pl.BlockSpec(a.shape, lambda bi, ti: (0,) * a.ndim)
    st_spec = pl.BlockSpec((1, 1, ns), lambda bi, ti: (bi, 0, 0))
    seg = tt // 8
    t_last = (valid_len - 1) % tt
    vm = lambda r: pltpu.VMEM((r, ns), F32)
    return pl.pallas_call(
        functools.partial(_s5_kernel, last_row=8 * (t_last % seg) + t_last // seg, bf16=bf16),
        grid=(b, lpad // tt),
        in_specs=[pl.BlockSpec((1, tt, d), lambda bi, ti: (bi, ti, 0)), st_spec, st_spec]
                 + [pspec(a) for a in plist],
        out_specs=[pl.BlockSpec((1, tt, d), lambda bi, ti: (bi, ti, 0)), st_spec, st_spec],
        out_shape=[jax.ShapeDtypeStruct((b, lpad, d), F32), jax.ShapeDtypeStruct((b, 1, ns), F32),
                   jax.ShapeDtypeStruct((b, 1, ns), F32)],
        scratch_shapes=[vm(1), vm(1), vm(1), vm(1), vm(tt), vm(tt), vm(1), vm(1), vm(tt), vm(tt),
                        pltpu.VMEM((tt, tt), BF16)],
        compiler_params=_params(("arbitrary", "arbitrary")),
        name="s5",
    )(u, h0r, h0i, *plist)


def _pool_kernel(u_ref, hist_ref, wp_ref, sc_ref, o_ref, ext_ref, *, pos0):
    i = pl.program_id(1)
    tt = u_ref.shape[1]
    hpad = 16

    @pl.when(i == 0)
    def _():
        ext_ref[0:hpad, :] = hist_ref[0]

    u = u_ref[0]
    ext_ref[hpad:hpad + tt, :] = u
    grp = lax.broadcasted_iota(jnp.int32, (1, D_GRP), 1) // (D_GRP // len(POOL_WINDOWS))
    pos = pos0 + i * tt + lax.broadcasted_iota(jnp.int32, (tt, 1), 0)
    acc = u
    pooled = jnp.zeros_like(u)
    d = 1
    for gi, w in enumerate(POOL_WINDOWS):
        while d < w:
            acc = acc + ext_ref[hpad - d:hpad - d + tt, :]
            d += 1
        cnt = jnp.minimum(pos + 1, w).astype(F32)
        pooled = jnp.where(grp == gi, acc / cnt, pooled)
    pooled = pooled - u
    o_ref[0] = _mm(pooled, wp_ref[...], False) * sc_ref[...]
    ext_ref[0:hpad, :] = ext_ref[tt:tt + hpad, :]


def _pool(u, hist16, lp, pos0):
    b, lpad, d = u.shape
    tt = _row_tile(lpad, 512)
    nw = len(POOL_WINDOWS)
    gw = d // nw
    wp = jnp.einsum('gcd,gh->gchd', lp['w_pool'], jnp.eye(nw, dtype=F32)).reshape(d, d)
    return pl.pallas_call(
        functools.partial(_pool_kernel, pos0=pos0),
        grid=(b, lpad // tt),
        in_specs=[pl.BlockSpec((1, tt, d), lambda bi, ti: (bi, ti, 0)),
                  pl.BlockSpec((1, 16, d), lambda bi, ti: (bi, 0, 0)),
                  pl.BlockSpec((d, d), lambda bi, ti: (0, 0)),
                  pl.BlockSpec((1, d), lambda bi, ti: (0, 0))],
        out_specs=pl.BlockSpec((1, tt, d), lambda bi, ti: (bi, ti, 0)),
        out_shape=jax.ShapeDtypeStruct((b, lpad, d), F32),
        scratch_shapes=[pltpu.VMEM((tt + 16, d), F32)],
        compiler_params=_params(("parallel", "arbitrary")),
        name="pool",
    )(u, hist16, wp, lp['pool_scale'].reshape(1, d))


def _bd_from_heads(s):
    b = s.shape[0]
    return jnp.einsum('bhvk,hg->bhvgk', s, jnp.eye(N_HEADS, dtype=s.dtype)).reshape(b, D_GRP, D_GRP)


def _heads_from_bd(s):
    b = s.shape[0]
    s5 = s.reshape(b, N_HEADS, HEAD_DIM, N_HEADS, HEAD_DIM)
    return jnp.stack([s5[:, h, :, h, :] for h in range(N_HEADS)], axis=1)


def _pad_rows(a, lpad):
    return a if a.shape[1] == lpad else jnp.pad(a, ((0, 0), (0, lpad - a.shape[1]), (0, 0)))


def _run_trunk(x, pos0, cache_k, cache_v, page_table, wkv0, shift0, ssm_re0, ssm_im0, pool0, P, final_g,
               bf16_small):
    bsz, l, d = x.shape
    depth = P['w_in'].shape[0]
    n = bsz * l
    lpad = -(-l // 8) * 8
    fd = shift0.shape[-1]
    widths = (D_GRP, D_GRP, D_GRP, fd, D_GRP, D_GRP)
    decode = cache_k is not None
    wdt = F32 if decode else BF16
    big = {name: P[name].astype(wdt) for name in ('w_in', 'w_out', 'w_gate', 'w_up', 'w_down')}
    if decode:
        page_size = cache_k.shape[2]
        assert (page_table.shape[1] * page_size) % MOBA_BLOCK == 0 and page_table.shape[1] * page_size >= MOBA_TOPK * MOBA_BLOCK
        ck = jnp.transpose(cache_k, (0, 1, 3, 4, 2))
        cv = jnp.transpose(cache_v, (0, 1, 3, 4, 2))
    else:
        kv_t = (jnp.zeros((depth, bsz, D_GRP, l), F32),) * 2
    xf = x.reshape(n, d)
    outs = []
    for li in range(depth):
        lp = {name: arr[li] for name, arr in P.items()}
        if decode:
            q, k, v, f, us, up = _inproj(xf, P['norm1_g'], big['w_in'], li, widths)
            top = _select_blocks(page_table, q, ck, li)
            att = _attn_decode(top, page_table, q, k, v, ck, cv, li)
            kv_new = (k.reshape(bsz, l, N_HEADS, HEAD_DIM), v.reshape(bsz, l, N_HEADS, HEAD_DIM))
        else:
            q, f, us, up, kb, vb, km, *kv_t = _inproj(xf, P['norm1_g'], big['w_in'], li, widths, kv_t)
            kv_new = ()
            att = _attn_prompt(q.reshape(bsz, l, D_GRP), kb.reshape(bsz, l, D_GRP), vb.reshape(bsz, l, D_GRP),
                               km.reshape(bsz, l // MOBA_BLOCK, D_GRP)).reshape(n, D_GRP)
        f3 = f.reshape(bsz, l, fd)
        rw, s_new = _rwkv(_pad_rows(f3, lpad), shift0[li], _bd_from_heads(wkv0[li]), lp, l, bf16_small)
        g_n = ssm_re0.shape[2] * ssm_re0.shape[3]
        us3 = us.reshape(bsz, l, D_GRP)
        ss, hr, hi = _s5(us3 if l == 1 else _pad_rows(us3, lpad), ssm_re0[li].reshape(bsz, 1, g_n),
                         ssm_im0[li].reshape(bsz, 1, g_n), lp, l, bf16_small)
        up3 = up.reshape(bsz, l, D_GRP)
        hist16 = jnp.pad(pool0[li], ((0, 0), (1, 0), (0, 0)))
        pm = _pool(_pad_rows(up3, lpad), hist16, lp, pos0)
        xf, = _post(att, rw[:, :l].reshape(n, D_GRP), ss[:, :l].reshape(n, D_GRP),
                    pm[:, :l].reshape(n, D_GRP), xf, P['mix_g'], big['w_out'], P['norm2_g'],
                    big['w_gate'], big['w_up'], big['w_down'], final_g, li, li == depth - 1)
        pool_new = jnp.concatenate([pool0[li], up3], axis=1)[:, -POOL_HIST:]
        outs.append(kv_new + (_heads_from_bd(s_new), f3[:, -1], hr.reshape(ssm_re0.shape[1:]),
                              hi.reshape(ssm_im0.shape[1:]), pool_new))
    new_state = [jnp.stack(s, axis=0) for s in zip(*outs)]
    if not decode:
        new_state = [t.reshape(depth, bsz, N_HEADS, HEAD_DIM, l).transpose(0, 1, 4, 2, 3) for t in kv_t] + new_state
    return xf.reshape(bsz, l, d), new_state


def kernel(x_prompt, x_sample, cache_k, cache_v, page_table, state_wkv, state_shift, state_ssm_re, state_ssm_im, state_pool, norm1_g, w_in, mu_shift, w0, w2, a0, a2, g2, k_k, k_a, r_k, lnx_g, lnx_b, lam_re, lam_im, log_dt, b_re, b_im, c_re, c_im, d_skip, w_glu, b_glu, w_pool, pool_scale, mix_g, w_out, norm2_g, w_gate, w_up, w_down, final_g):
    P = dict(norm1_g=norm1_g, w_in=w_in, mu_shift=mu_shift, w0=w0, w2=w2, a0=a0, a2=a2, g2=g2,
             k_k=k_k, k_a=k_a, r_k=r_k, lnx_g=lnx_g, lnx_b=lnx_b, lam_re=lam_re, lam_im=lam_im,
             log_dt=log_dt, b_re=b_re, b_im=b_im, c_re=c_re, c_im=c_im, d_skip=d_skip, w_glu=w_glu,
             b_glu=b_glu, w_pool=w_pool, pool_scale=pool_scale, mix_g=mix_g, w_out=w_out,
             norm2_g=norm2_g, w_gate=w_gate, w_up=w_up, w_down=w_down)
    dt = x_prompt.dtype
    bp = x_prompt.shape[0]
    depth = w_in.shape[0]
    z = lambda *s: jnp.zeros((depth, bp) + s, dt)
    y_p, (k_p, v_p, wkv_p, shift_p, sre_p, sim_p, pool_p) = _run_trunk(
        x_prompt, 0, None, None, None, z(*state_wkv.shape[2:]), z(state_shift.shape[2]),
        z(*state_ssm_re.shape[2:]), z(*state_ssm_im.shape[2:]), z(*state_pool.shape[2:]), P, final_g, True)
    past_len = page_table.shape[1] * cache_k.shape[2]
    y_s, (k_s, v_s, wkv_s, shift_s, sre_s, sim_s, pool_s) = _run_trunk(
        x_sample, past_len, cache_k, cache_v, page_table, state_wkv, state_shift, state_ssm_re,
        state_ssm_im, state_pool, P, final_g, False)
    return (y_p, y_s, k_p, v_p, k_s, v_s, wkv_p, wkv_s, shift_p, shift_s,
            sre_p, sim_p, sre_s, sim_s, pool_p, pool_s)
```
